```python
import math
import jax, jax.numpy as jnp
from jax import lax
import numpy as np

D_MODEL = 4096
BATCH = 2
SEQ = 8192
DEPTH = 1

MEM_LEN = 256
A_HEADS = 32
A_KV_HEADS = 4
A_HEAD_DIM = 64
WINDOW = 128
BLOCK = 128
N_BUCKETS = 32
MAX_DISTANCE = 128
B_HEADS = 16
B_DK = 128
B_DV = 128
CONV_W = 4
CHUNK = 64
M_HEADS = 4
M_HEAD_DIM = 128
D_FF = -(-8 * D_MODEL // (3 * 256)) * 256

A_Q = A_HEADS * A_HEAD_DIM
A_KV = A_KV_HEADS * A_HEAD_DIM
B_QK = B_HEADS * B_DK
B_V = B_HEADS * B_DV
B_CONV = 2 * B_QK + B_V
M_Q = M_HEADS * M_HEAD_DIM
IN_SIZES = (A_Q, A_KV, A_KV, B_CONV, B_V, B_HEADS, B_HEADS, M_Q, D_MODEL, D_MODEL, D_MODEL)
N_IN = sum(IN_SIZES)
EPS = 1e-6
NEG = -1e30

kernel_name = 'hybrid_swa_sink_gdn_memory_block'


def rms_norm(x, g):
    xf = x.astype(jnp.float32)
    y = xf * lax.rsqrt(jnp.mean(xf * xf, axis=-1, keepdims=True) + EPS)
    return (y * g.astype(jnp.float32)).astype(x.dtype)


def l2_norm(x):
    return x * lax.rsqrt(jnp.sum(x * x, axis=-1, keepdims=True) + EPS)


def split_cols(t, sizes):
    out = []
    off = 0
    for n in sizes:
        out.append(t[..., off:off + n])
        off += n
    return out


def t5_bucket(n):
    max_exact = N_BUCKETS // 2
    nf = jnp.maximum(n, 1).astype(jnp.float32)
    large = max_exact + (jnp.log(nf / max_exact) / math.log(MAX_DISTANCE / max_exact)
                         * (N_BUCKETS - max_exact)).astype(jnp.int32)
    large = jnp.minimum(large, N_BUCKETS - 1)
    return jnp.where(n < max_exact, n, large)


def swa_sink_attention(q, k, v, sink, rel_bias):
    bsz, s_len = q.shape[0], q.shape[1]
    nb = s_len // BLOCK
    grp = A_HEADS // A_KV_HEADS
    qb = q.reshape(bsz, nb, BLOCK, A_KV_HEADS, grp, A_HEAD_DIM)
    kb = k.reshape(bsz, nb, BLOCK, A_KV_HEADS, A_HEAD_DIM)
    vb = v.reshape(bsz, nb, BLOCK, A_KV_HEADS, A_HEAD_DIM)

    def with_prev(t):
        prev = jnp.pad(t[:, :-1], ((0, 0), (1, 0), (0, 0), (0, 0), (0, 0)))
        return jnp.concatenate([prev, t], axis=2)

    kw, vw = with_prev(kb), with_prev(vb)
    s = jnp.einsum('bnqhgd,bnkhd->bnhgqk', qb, kw).astype(jnp.float32) * (A_HEAD_DIM ** -0.5)
    qi = jnp.arange(BLOCK)[:, None]
    kj = jnp.arange(2 * BLOCK)[None, :]
    dist = qi + BLOCK - kj
    band = (dist >= 0) & (dist < WINDOW)
    exists = (jnp.arange(nb)[:, None, None] > 0) | (kj >= BLOCK)[None]
    mask = band[None] & exists
    bias = rel_bias[t5_bucket(jnp.maximum(dist, 0))].astype(jnp.float32)
    bias = bias.transpose(2, 0, 1).reshape(A_KV_HEADS, grp, BLOCK, 2 * BLOCK)
    s = jnp.where(mask[None, :, None, None], s + bias, NEG)
    sk = sink.astype(jnp.float32).reshape(A_KV_HEADS, grp)[:, :, None, None]
    m = jnp.maximum(jnp.max(s, axis=-1, keepdims=True), sk)
    p = jnp.exp(s - m)
    p = p / (jnp.sum(p, axis=-1, keepdims=True) + jnp.exp(sk - m))
    o = jnp.einsum('bnhgqk,bnkhd->bnqhgd', p.astype(v.dtype), vw)
    return o.reshape(bsz, s_len, A_Q)


def causal_conv(x, w):
    s_len = x.shape[1]
    xp = jnp.pad(x, ((0, 0), (CONV_W - 1, 0), (0, 0)))
    y = xp[:, 0:s_len] * w[0]
    for i in range(1, CONV_W):
        y = y + xp[:, i:i + s_len] * w[i]
    return y


def gated_delta_rule(q, k, v, g, beta):
    bsz, s_len, nh, dk = q.shape
    dv = v.shape[-1]
    n = s_len // CHUNK

    def chunk(t):
        return t.reshape(bsz, n, CHUNK, nh, -1).transpose(0, 3, 1, 2, 4)

    q, k, v = chunk(q), chunk(k), chunk(v)
    g = g.reshape(bsz, n, CHUNK, nh).transpose(0, 3, 1, 2)
    beta = beta.reshape(bsz, n, CHUNK, nh).transpose(0, 3, 1, 2)
    gc = jnp.cumsum(g, axis=-1)
    tri = jnp.tril(jnp.ones((CHUNK, CHUNK), dtype=bool))
    strict = jnp.tril(jnp.ones((CHUNK, CHUNK), dtype=bool), -1)
    decay = jnp.exp(jnp.where(tri, gc[..., :, None] - gc[..., None, :], -jnp.inf))
    kb = k * beta[..., None]
    a_mat = jnp.where(strict, jnp.einsum('bhncd,bhnsd->bhncs', kb, k) * decay, 0.0)
    rhs = jnp.concatenate([v * beta[..., None], kb * jnp.exp(gc)[..., None]], axis=-1)
    sol = lax.linalg.triangular_solve(a_mat + jnp.eye(CHUNK, dtype=a_mat.dtype), rhs,
                                      left_side=True, lower=True)
    u, w = sol[..., :dv], sol[..., dv:]
    att = jnp.where(tri, jnp.einsum('bhncd,bhnsd->bhncs', q, k) * decay, 0.0)

    def step(state, inp):
        q_c, k_c, u_c, w_c, att_c, gc_c = inp
        v_new = u_c - jnp.einsum('bhck,bhkv->bhcv', w_c, state)
        o_c = (jnp.einsum('bhck,bhkv->bhcv', q_c * jnp.exp(gc_c)[..., None], state)
               + jnp.einsum('bhcs,bhsv->bhcv', att_c, v_new))
        g_last = gc_c[..., -1:]
        state = (state * jnp.exp(g_last)[..., None]
                 + jnp.einsum('bhck,bhcv->bhkv', k_c * jnp.exp(g_last - gc_c)[..., None], v_new))
        return state, o_c

    xs = tuple(jnp.moveaxis(t, 2, 0) for t in (q, k, u, w, att, gc))
    state0 = jnp.zeros((bsz, nh, dk, dv), jnp.float32)
    _, o = lax.scan(step, state0, xs)
    return o.transpose(1, 0, 3, 2, 4).reshape(bsz, s_len, nh, dv)


def setup_inputs(seed: int = 0) -> dict:
    key = jax.random.key(seed)
    ks = jax.random.split(key, 24)
    f32 = jnp.float32

    def nrm(k, shape, fan_in):
        return jax.random.normal(k, shape, f32) * (fan_in ** -0.5)

    def gain(k, shape):
        return 1.0 + 0.02 * jax.random.normal(k, shape, f32)

    dt = jnp.exp(jax.random.uniform(ks[5], (DEPTH, B_HEADS), f32, math.log(1e-3), math.log(0.1)))
    return {
        'x': jax.random.normal(ks[0], (BATCH, SEQ, D_MODEL), f32),
        'mem': jax.random.normal(ks[1], (BATCH, MEM_LEN, D_MODEL), f32),
        'rel_bias': 0.1 * jax.random.normal(ks[2], (N_BUCKETS, A_HEADS), f32),
        'g_mix': gain(ks[3], (DEPTH, D_MODEL)),
        'w_in': nrm(ks[4], (DEPTH, D_MODEL, N_IN), D_MODEL),
        'conv_w': nrm(ks[6], (DEPTH, CONV_W, B_CONV), CONV_W),
        'a_log': jnp.log(jax.random.uniform(ks[7], (DEPTH, B_HEADS), f32, 1.0, 16.0)),
        'dt_bias': dt + jnp.log(-jnp.expm1(-dt)),
        'g_dn_out': gain(ks[8], (DEPTH, B_DV)),
        'sinks': 0.5 * jax.random.normal(ks[9], (DEPTH, A_HEADS), f32),
        'g_mem': gain(ks[10], (DEPTH, D_MODEL)),
        'w_mem_kv': nrm(ks[11], (DEPTH, D_MODEL, 2 * M_Q), D_MODEL),
        'w_br_a': nrm(ks[12], (DEPTH, A_Q, D_MODEL), A_Q),
        'w_br_b': nrm(ks[13], (DEPTH, B_V, D_MODEL), B_V),
        'w_br_m': nrm(ks[14], (DEPTH, M_Q, D_MODEL), M_Q),
        'w_o': nrm(ks[15], (DEPTH, D_MODEL, D_MODEL), D_MODEL),
        'g_ffn': gain(ks[16], (DEPTH, D_MODEL)),
        'w_ffn_in': nrm(ks[17], (DEPTH, D_MODEL, 2 * D_FF), D_MODEL),
        'w_ffn_out': nrm(ks[18], (DEPTH, D_FF, D_MODEL), D_FF),
        'g_final': gain(ks[19], (D_MODEL,)),
    }


def reference(x, mem, rel_bias, g_mix, w_in, conv_w, a_log, dt_bias, g_dn_out, sinks, g_mem,
              w_mem_kv, w_br_a, w_br_b, w_br_m, w_o, g_ffn, w_ffn_in, w_ffn_out, g_final):
    f32 = jnp.float32
    bsz, s_len, _ = x.shape
    for l in range(DEPTH):
        h = rms_norm(x, g_mix[l])
        proj = h @ w_in[l]
        a_q, a_k, a_v, b_qkv, b_z, b_b, b_a, m_q, gt_a, gt_b, gt_m = split_cols(proj, IN_SIZES)

        o_a = swa_sink_attention(a_q.reshape(bsz, s_len, A_HEADS, A_HEAD_DIM),
                                 a_k.reshape(bsz, s_len, A_KV_HEADS, A_HEAD_DIM),
                                 a_v.reshape(bsz, s_len, A_KV_HEADS, A_HEAD_DIM),
                                 sinks[l], rel_bias)

        qkv = jax.nn.silu(causal_conv(b_qkv, conv_w[l])).astype(f32)
        bq, bk, bv = split_cols(qkv, (B_QK, B_QK, B_V))
        bq = l2_norm(bq.reshape(bsz, s_len, B_HEADS, B_DK)) * (B_DK ** -0.5)
        bk = l2_norm(bk.reshape(bsz, s_len, B_HEADS, B_DK))
        bv = bv.reshape(bsz, s_len, B_HEADS, B_DV)
        beta = jax.nn.sigmoid(b_b.astype(f32))
        g = -jnp.exp(a_log[l].astype(f32)) * jax.nn.softplus(b_a.astype(f32) + dt_bias[l].astype(f32))
        o_b = gated_delta_rule(bq, bk, bv, g, beta)
        o_b = rms_norm(o_b, g_dn_out[l]) * jax.nn.silu(b_z.reshape(bsz, s_len, B_HEADS, B_DV).astype(f32))
        o_b = o_b.reshape(bsz, s_len, B_V).astype(x.dtype)

        mk, mv = split_cols(rms_norm(mem, g_mem[l]) @ w_mem_kv[l], (M_Q, M_Q))
        mk = mk.reshape(bsz, MEM_LEN, M_HEADS, M_HEAD_DIM)
        mv = mv.reshape(bsz, MEM_LEN, M_HEADS, M_HEAD_DIM)
        sm = jnp.einsum('bshd,bmhd->bhsm', m_q.reshape(bsz, s_len, M_HEADS, M_HEAD_DIM), mk)
        pm = jax.nn.softmax(sm.astype(f32) * (M_HEAD_DIM ** -0.5), axis=-1).astype(x.dtype)
        o_m = jnp.einsum('bhsm,bmhd->bshd', pm, mv).reshape(bsz, s_len, M_Q)

        y = (jax.nn.sigmoid(gt_a) * (o_a @ w_br_a[l])
             + jax.nn.sigmoid(gt_b) * (o_b @ w_br_b[l])
             + jax.nn.sigmoid(gt_m) * (o_m @ w_br_m[l]))
        x = x + y @ w_o[l]

        gate, up = split_cols(rms_norm(x, g_ffn[l]) @ w_ffn_in[l], (D_FF, D_FF))
        x = x + (jax.nn.silu(gate) * up) @ w_ffn_out[l]
    return rms_norm(x, g_final)
```

```python
import functools
import math

import numpy as np
import jax
import jax.numpy as jnp
from jax import lax
from jax.experimental import pallas as pl
from jax.experimental.pallas import tpu as pltpu

F32 = jnp.float32
BF16 = jnp.bfloat16

EPS = 1e-6
NEG = -1e30

A_HEADS = 32
A_KV_HEADS = 4
A_HEAD_DIM = 64
A_GROUP = A_HEADS // A_KV_HEADS
WINDOW = 128
BLOCK = 128
N_BUCKETS = 32
MAX_DISTANCE = 128
B_HEADS = 16
B_DK = 128
B_DV = 128
CONV_W = 4
CHUNK = 64
SUB = 16
M_HEADS = 4
M_HEAD_DIM = 128

A_Q = A_HEADS * A_HEAD_DIM
A_KV = A_KV_HEADS * A_HEAD_DIM
B_QK = B_HEADS * B_DK
B_V = B_HEADS * B_DV
M_Q = M_HEADS * M_HEAD_DIM

LANES = 128
SUBLANES = 8
VMEM_LIMIT = 56 * 1024 * 1024

OFF_AQ = 0
OFF_AK = OFF_AQ + A_Q
OFF_AV = OFF_AK + A_KV
OFF_BQ = OFF_AV + A_KV
OFF_BK = OFF_BQ + B_QK
OFF_BV = OFF_BK + B_QK
OFF_BZ = OFF_BV + B_V
OFF_MQ = OFF_BZ + B_V


def _params(sem):
    return pltpu.CompilerParams(dimension_semantics=sem, vmem_limit_bytes=VMEM_LIMIT)


def _sigmoid(x):
    return 1.0 / (1.0 + jnp.exp(-x))


def _mm(a, b):
    return jnp.dot(a.astype(BF16), b.astype(BF16), preferred_element_type=F32)


def _mm_nt(a, b):
    return lax.dot_general(a.astype(BF16), b.astype(BF16), (((1,), (1,)), ((), ())),
                           preferred_element_type=F32)


def _mm_tn(a, b):
    return lax.dot_general(a.astype(BF16), b.astype(BF16), (((0,), (0,)), ((), ())),
                           preferred_element_type=F32)


def _rmsnorm_kernel(x_ref, g_ref, o_ref):
    x = x_ref[...]
    r = lax.rsqrt(jnp.mean(x * x, axis=-1, keepdims=True) + EPS)
    o_ref[...] = ((x * r) * g_ref[...]).astype(o_ref.dtype)


def _rmsnorm(x, g, out_dtype, rows=256):
    t, d = x.shape
    rows = min(rows, t)
    return pl.pallas_call(
        _rmsnorm_kernel,
        grid=(t // rows,),
        in_specs=[pl.BlockSpec((rows, d), lambda i: (i, 0)),
                  pl.BlockSpec((1, d), lambda i: (0, 0))],
        out_specs=pl.BlockSpec((rows, d), lambda i: (i, 0)),
        out_shape=jax.ShapeDtypeStruct((t, d), out_dtype),
        compiler_params=_params(("parallel",)),
        name="rmsnorm",
    )(x, g.reshape(1, d))


def _rmsnorm_ba_kernel(x_ref, g_ref, wba_ref, h_ref, ba_ref):
    x = x_ref[...]
    r = lax.rsqrt(jnp.mean(x * x, axis=-1, keepdims=True) + EPS)
    h = ((x * r) * g_ref[...]).astype(BF16)
    h_ref[...] = h
    ba_ref[...] = jnp.dot(h, wba_ref[...], preferred_element_type=F32)


def _rmsnorm_ba(x, g, w_ba, rows=256):
    t, d = x.shape
    return pl.pallas_call(
        _rmsnorm_ba_kernel,
        grid=(t // rows,),
        in_specs=[pl.BlockSpec((rows, d), lambda i: (i, 0)),
                  pl.BlockSpec((1, d), lambda i: (0, 0)),
                  pl.BlockSpec((d, LANES), lambda i: (0, 0))],
        out_specs=[pl.BlockSpec((rows, d), lambda i: (i, 0)),
                   pl.BlockSpec((rows, LANES), lambda i: (i, 0))],
        out_shape=[jax.ShapeDtypeStruct((t, d), BF16),
                   jax.ShapeDtypeStruct((t, LANES), F32)],
        compiler_params=_params(("parallel",)),
        name="rmsnorm_ba",
    )(x, g.reshape(1, d), w_ba)


def _mm_kernel(x_ref, w_ref, o_ref):
    o_ref[...] = jnp.dot(x_ref[...], w_ref[...], preferred_element_type=F32).astype(o_ref.dtype)


def _matmul(x, w, out_dtype, bm=1024, bn=1024):
    m, k = x.shape
    n = w.shape[1]
    bm = min(bm, m)
    return pl.pallas_call(
        _mm_kernel,
        grid=(m // bm, n // bn),
        in_specs=[pl.BlockSpec((bm, k), lambda i, j: (i, 0)),
                  pl.BlockSpec((k, bn), lambda i, j: (0, j))],
        out_specs=pl.BlockSpec((bm, bn), lambda i, j: (i, j)),
        out_shape=jax.ShapeDtypeStruct((m, n), out_dtype),
        compiler_params=_params(("parallel", "arbitrary")),
        name="matmul",
    )(x, w)


def _mm_res_kernel(x_ref, w_ref, res_ref, o_ref):
    o_ref[...] = res_ref[...] + jnp.dot(x_ref[...], w_ref[...], preferred_element_type=F32)


def _matmul_residual(x, w, res, bm=1024, bn=1024):
    m, k = x.shape
    n = w.shape[1]
    bm = min(bm, m)
    return pl.pallas_call(
        _mm_res_kernel,
        grid=(m // bm, n // bn),
        in_specs=[pl.BlockSpec((bm, k), lambda i, j: (i, 0)),
                  pl.BlockSpec((k, bn), lambda i, j: (0, j)),
                  pl.BlockSpec((bm, bn), lambda i, j: (i, j))],
        out_specs=pl.BlockSpec((bm, bn), lambda i, j: (i, j)),
        out_shape=jax.ShapeDtypeStruct((m, n), F32),
        compiler_params=_params(("parallel", "arbitrary")),
        name="matmul_residual",
    )(x, w, res)


def _mm_res_acc_kernel(x_ref, w_ref, res_ref, o_ref, acc_ref, *, nk):
    kk = pl.program_id(2)

    @pl.when(kk == 0)
    def _():
        acc_ref[...] = jnp.zeros_like(acc_ref)

    acc_ref[...] += jnp.dot(x_ref[...], w_ref[...], preferred_element_type=F32)

    @pl.when(kk == nk - 1)
    def _():
        o_ref[...] = res_ref[...] + acc_ref[...]


def _matmul_residual_ksplit(x, w, res, bk, bm=1024, bn=1024):
    m, k = x.shape
    n = w.shape[1]
    bm = min(bm, m)
    nk = k // bk
    return pl.pallas_call(
        functools.partial(_mm_res_acc_kernel, nk=nk),
        grid=(m // bm, n // bn, nk),
        in_specs=[pl.BlockSpec((bm, bk), lambda i, j, l: (i, l)),
                  pl.BlockSpec((bk, bn), lambda i, j, l: (l, j)),
                  pl.BlockSpec((bm, bn), lambda i, j, l: (i, j))],
        out_specs=pl.BlockSpec((bm, bn), lambda i, j, l: (i, j)),
        out_shape=jax.ShapeDtypeStruct((m, n), F32),
        scratch_shapes=[pltpu.VMEM((bm, bn), F32)],
        compiler_params=_params(("parallel", "parallel", "arbitrary")),
        name="matmul_residual_ksplit",
    )(x, w, res)


def _swiglu_in_kernel(x_ref, wg_ref, wu_ref, o_ref):
    x = x_ref[...]
    g = jnp.dot(x, wg_ref[...], preferred_element_type=F32)
    u = jnp.dot(x, wu_ref[...], preferred_element_type=F32)
    o_ref[...] = ((g * _sigmoid(g)) * u).astype(o_ref.dtype)


def _swiglu_in(x, w, n_ff, bm=1024, bn=512):
    m, k = x.shape
    bm = min(bm, m)
    nb = n_ff // bn
    return pl.pallas_call(
        _swiglu_in_kernel,
        grid=(m // bm, nb),
        in_specs=[pl.BlockSpec((bm, k), lambda i, j: (i, 0)),
                  pl.BlockSpec((k, bn), lambda i, j: (0, j)),
                  pl.BlockSpec((k, bn), lambda i, j: (0, j + nb))],
        out_specs=pl.BlockSpec((bm, bn), lambda i, j: (i, j)),
        out_shape=jax.ShapeDtypeStruct((m, n_ff), BF16),
        compiler_params=_params(("parallel", "arbitrary")),
        name="swiglu_in",
    )(x, w, w)


def _merge_kernel(oa_ref, ob_ref, om_ref, wa_ref, wb_ref, wm_ref, ga_ref, gb_ref, gm_ref, y_ref):
    ya = jnp.dot(oa_ref[...], wa_ref[...], preferred_element_type=F32)
    y = _sigmoid(ga_ref[...].astype(F32)) * ya
    yb = jnp.dot(ob_ref[...], wb_ref[...], preferred_element_type=F32)
    y = y + _sigmoid(gb_ref[...].astype(F32)) * yb
    ym = jnp.dot(om_ref[...], wm_ref[...], preferred_element_type=F32)
    y = y + _sigmoid(gm_ref[...].astype(F32)) * ym
    y_ref[...] = y.astype(y_ref.dtype)


def _merge(o_a, o_b, o_m, w_a, w_b, w_m, proj, off_ga, off_gb, off_gm, bm=1024, bn=512):
    m = o_a.shape[0]
    n = w_a.shape[1]
    bm = min(bm, m)
    ja, jb, jm = off_ga // bn, off_gb // bn, off_gm // bn

    def row(width):
        return pl.BlockSpec((bm, width), lambda i, j: (i, 0))

    def col(height):
        return pl.BlockSpec((height, bn), lambda i, j: (0, j))

    def gate(j0):
        return pl.BlockSpec((bm, bn), lambda i, j: (i, j0 + j))

    return pl.pallas_call(
        _merge_kernel,
        grid=(m // bm, n // bn),
        in_specs=[row(o_a.shape[1]), row(o_b.shape[1]), row(o_m.shape[1]),
                  col(w_a.shape[0]), col(w_b.shape[0]), col(w_m.shape[0]),
                  gate(ja), gate(jb), gate(jm)],
        out_specs=pl.BlockSpec((bm, bn), lambda i, j: (i, j)),
        out_shape=jax.ShapeDtypeStruct((m, n), BF16),
        compiler_params=_params(("parallel", "arbitrary")),
        name="merge",
    )(o_a, o_b, o_m, w_a, w_b, w_m, proj, proj, proj)


def _t5_bucket_table():
    qi = np.arange(BLOCK)[:, None]
    kj = np.arange(2 * BLOCK)[None, :]
    dist = qi + BLOCK - kj
    band = (dist >= 0) & (dist < WINDOW)
    n = np.maximum(dist, 0)
    max_exact = N_BUCKETS // 2
    nf = np.maximum(n, 1).astype(np.float32)
    large = max_exact + (np.log(nf / np.float32(max_exact)) / np.float32(math.log(MAX_DISTANCE / max_exact))
                         * np.float32(N_BUCKETS - max_exact)).astype(np.int32)
    large = np.minimum(large, N_BUCKETS - 1)
    bucket = np.where(n < max_exact, n, large)
    return np.where(band, bucket, -1).astype(np.int32)


def _bias_table_kernel(rel_ref, idx_ref, o_ref):
    h = pl.program_id(0)
    idx = idx_ref[...]
    acc = jnp.full(idx.shape, NEG, F32)
    for b in range(N_BUCKETS):
        acc = jnp.where(idx == b, rel_ref[b, h], acc)
    o_ref[0] = acc


def _bias_table(rel_bias):
    idx = jnp.asarray(_t5_bucket_table())
    return pl.pallas_call(
        _bias_table_kernel,
        grid=(A_HEADS,),
        in_specs=[pl.BlockSpec(memory_space=pltpu.SMEM),
                  pl.BlockSpec((BLOCK, 2 * BLOCK), lambda h: (0, 0))],
        out_specs=pl.BlockSpec((1, BLOCK, 2 * BLOCK), lambda h: (h, 0, 0)),
        out_shape=jax.ShapeDtypeStruct((A_HEADS, BLOCK, 2 * BLOCK), F32),
        compiler_params=_params(("arbitrary",)),
        name="swa_bias_table",
    )(rel_bias, idx)


def _swa_kernel(sink_ref, q_ref, kc_ref, kp_ref, vc_ref, vp_ref, bias_ref, o_ref, *, nb):
    t = pl.program_id(0)
    first = (t % nb) == 0
    col = lax.broadcasted_iota(jnp.int32, (BLOCK, 2 * BLOCK), 1)
    missing = jnp.logical_and(first, col < BLOCK)
    scale = A_HEAD_DIM ** -0.5
    for kh in range(A_KV_HEADS):
        ks = slice(kh * A_HEAD_DIM, (kh + 1) * A_HEAD_DIM)
        kw = jnp.concatenate([kp_ref[:, ks], kc_ref[:, ks]], axis=0)
        vw = jnp.concatenate([vp_ref[:, ks], vc_ref[:, ks]], axis=0)
        for g in range(A_GROUP):
            h = kh * A_GROUP + g
            hs = slice(h * A_HEAD_DIM, (h + 1) * A_HEAD_DIM)
            s = _mm_nt(q_ref[:, hs], kw) * scale + bias_ref[h]
            s = jnp.where(missing, NEG, s)
            sk = sink_ref[h]
            m = jnp.maximum(jnp.max(s, axis=-1, keepdims=True), sk)
            p = jnp.exp(s - m)
            denom = jnp.sum(p, axis=-1, keepdims=True) + jnp.exp(sk - m)
            o = _mm(p, vw) * (1.0 / denom)
            o_ref[:, hs] = o.astype(o_ref.dtype)


def _swa_attention(proj, sinks, bias, seq):
    t = proj.shape[0]
    nb = seq // BLOCK
    jk, jv = OFF_AK // A_KV, OFF_AV // A_KV

    def cur(jc):
        return pl.BlockSpec((BLOCK, A_KV), lambda i: (i, jc))

    def prev(jc):
        return pl.BlockSpec((BLOCK, A_KV), lambda i: (jnp.maximum(i - 1, 0), jc))

    return pl.pallas_call(
        functools.partial(_swa_kernel, nb=nb),
        grid=(t // BLOCK,),
        in_specs=[pl.BlockSpec(memory_space=pltpu.SMEM),
                  pl.BlockSpec((BLOCK, A_Q), lambda i: (i, OFF_AQ // A_Q)),
                  cur(jk), prev(jk), cur(jv), prev(jv),
                  pl.BlockSpec((A_HEADS, BLOCK, 2 * BLOCK), lambda i: (0, 0, 0))],
        out_specs=pl.BlockSpec((BLOCK, A_Q), lambda i: (i, 0)),
        out_shape=jax.ShapeDtypeStruct((t, A_Q), BF16),
        compiler_params=_params(("parallel",)),
        name="swa_attention",
    )(sinks, proj, proj, proj, proj, proj, bias)


def _xattn_kernel(q_ref, kv_ref, o_ref):
    scale = M_HEAD_DIM ** -0.5
    for h in range(M_HEADS):
        hs = slice(h * M_HEAD_DIM, (h + 1) * M_HEAD_DIM)
        vs = slice(M_Q + h * M_HEAD_DIM, M_Q + (h + 1) * M_HEAD_DIM)
        s = _mm_nt(q_ref[:, hs], kv_ref[:, hs]) * scale
        m = jnp.max(s, axis=-1, keepdims=True)
        p = jnp.exp(s - m)
        denom = jnp.sum(p, axis=-1, keepdims=True)
        o = _mm(p, kv_ref[:, vs]) * (1.0 / denom)
        o_ref[:, hs] = o.astype(o_ref.dtype)


def _cross_attention(proj, mem_kv, seq, mem_len, rows=512):
    t = proj.shape[0]
    rows = min(rows, seq)
    per_batch = seq // rows
    return pl.pallas_call(
        _xattn_kernel,
        grid=(t // rows,),
        in_specs=[pl.BlockSpec((rows, M_Q), lambda i: (i, OFF_MQ // M_Q)),
                  pl.BlockSpec((mem_len, 2 * M_Q), lambda i: (i // per_batch, 0))],
        out_specs=pl.BlockSpec((rows, M_Q), lambda i: (i, 0)),
        out_shape=jax.ShapeDtypeStruct((t, M_Q), BF16),
        compiler_params=_params(("parallel",)),
        name="cross_attention",
    )(proj, mem_kv)


def _unit_lower_inverse(a_mat, eye, diag_blocks):
    nd = jnp.where(diag_blocks, -a_mat, 0.0)
    a_off = jnp.where(diag_blocks, 0.0, a_mat)
    p = eye + nd
    pw = nd
    for _ in range(int(math.log2(SUB)) - 1):
        pw = _mm(pw, pw)
        p = p + _mm(p, pw)
    d_inv = p
    m1 = _mm(d_inv, a_off)
    m2 = _mm(m1, m1)
    assert CHUNK // SUB == 4
    r_minus_i = (m2 - m1) - _mm(m1, m2)
    return d_inv + _mm(r_minus_i, d_inv)


def _gdn_kernel(q_ref, k_ref, v_ref, z_ref, ba_ref, cwq_ref, cwk_ref, cwv_ref, alog_ref, dtb_ref, gout_ref,
                o_ref, xq_ref, xk_ref, xv_ref, s_ref, gt_ref, *, tt, hb):
    hg = pl.program_id(1)
    j = pl.program_id(2)
    pad = SUBLANES

    @pl.when(j == 0)
    def _():
        for r in (xq_ref, xk_ref, xv_ref):
            r[0:pad, :] = jnp.zeros((pad, r.shape[1]), F32)
        s_ref[...] = jnp.zeros_like(s_ref)

    @pl.when(j > 0)
    def _():
        for r in (xq_ref, xk_ref, xv_ref):
            r[0:pad, :] = r[tt:tt + pad, :]

    def conv_silu(x_ref, xe_ref, cw_ref):
        xe_ref[pad:pad + tt, :] = x_ref[...].astype(F32)
        w = cw_ref[...]
        y = xe_ref[pad - 3:pad - 3 + tt, :] * w[0:1, :]
        for i in range(1, CONV_W):
            y = y + xe_ref[pad - 3 + i:pad - 3 + i + tt, :] * w[i:i + 1, :]
        return y * _sigmoid(y)

    qs = conv_silu(q_ref, xq_ref, cwq_ref)
    ks = conv_silu(k_ref, xk_ref, cwk_ref)
    vs = conv_silu(v_ref, xv_ref, cwv_ref)

    ba = ba_ref[...]
    lane = lax.broadcasted_iota(jnp.int32, (tt, LANES), 1)
    pos = lax.broadcasted_iota(jnp.int32, (tt, LANES), 0) % CHUNK
    beta_all = _sigmoid(ba)
    xg = ba + dtb_ref[...]
    softplus = jnp.maximum(xg, 0.0) + jnp.log1p(jnp.exp(-jnp.abs(xg)))
    gc_all = -jnp.exp(alog_ref[...]) * softplus
    step = 1
    while step < CHUNK:
        gc_all = gc_all + jnp.where(pos >= step, pltpu.roll(gc_all, step, axis=0), 0.0)
        step *= 2
    gt_ref[...] = gc_all.T

    row = lax.broadcasted_iota(jnp.int32, (CHUNK, CHUNK), 0)
    colm = lax.broadcasted_iota(jnp.int32, (CHUNK, CHUNK), 1)
    tri = row >= colm
    strict = row > colm
    diag_blocks = (row // SUB) == (colm // SUB)
    eye = jnp.where(row == colm, 1.0, 0.0).astype(F32)

    def pick(arr, l):
        return jnp.sum(jnp.where(lane == l, arr, 0.0), axis=1, keepdims=True)

    for i in range(hb):
        h = hg * hb + i
        hs = slice(i * B_DK, (i + 1) * B_DK)
        q = qs[:, hs]
        k = ks[:, hs]
        v = vs[:, hs]
        q = (q * lax.rsqrt(jnp.sum(q * q, axis=-1, keepdims=True) + EPS)) * (B_DK ** -0.5)
        k = k * lax.rsqrt(jnp.sum(k * k, axis=-1, keepdims=True) + EPS)
        beta = pick(beta_all, h)
        gcc = pick(gc_all, B_HEADS + h)
        gcr = gt_ref[pl.ds(B_HEADS + h, 1), :]
        egc = jnp.exp(gcc)
        kb = k * beta
        vb = v * beta
        kbe = kb * egc
        qe = q * egc
        gw = gout_ref[...]
        zf = z_ref[:, hs].astype(F32)
        zg = zf * _sigmoid(zf)
        state = s_ref[i]
        for c in range(tt // CHUNK):
            rs = slice(c * CHUNK, (c + 1) * CHUNK)
            k_c = k[rs]
            gc_c = gcc[rs]
            decay = jnp.exp(jnp.where(tri, gc_c - gcr[:, rs], NEG))
            a_mat = jnp.where(strict, _mm_nt(kb[rs], k_c) * decay, 0.0)
            att = _mm_nt(q[rs], k_c) * decay
            t_minus_i = _unit_lower_inverse(a_mat, eye, diag_blocks) - eye
            u = vb[rs] + _mm(t_minus_i, vb[rs])
            w = kbe[rs] + _mm(t_minus_i, kbe[rs])
            ws = _mm(jnp.concatenate([w, qe[rs]], axis=0), state)
            v_new = u - ws[:CHUNK]
            o = ws[CHUNK:] + _mm(att, v_new)
            g_last = gc_c[CHUNK - 1:CHUNK, :]
            state = state * jnp.exp(g_last) + _mm_tn(k_c * jnp.exp(g_last - gc_c), v_new)
            o = (o * lax.rsqrt(jnp.mean(o * o, axis=-1, keepdims=True) + EPS)) * gw
            o_ref[rs, hs] = (o * zg[rs]).astype(o_ref.dtype)
        s_ref[i] = state


def _gated_deltanet(proj, ba, conv_w, a_log, dt_bias, g_out, batch, seq, tt=256, hb=2):
    t = proj.shape[0]
    tt = min(tt, seq)
    width = hb * B_DK
    n_t = seq // tt

    def tok(off):
        j0 = off // width
        return pl.BlockSpec((tt, width), lambda b, g, j: (b * n_t + j, j0 + g))

    def cw(off):
        j0 = off // width
        return pl.BlockSpec((CONV_W, width), lambda b, g, j: (0, j0 + g))

    vec = pl.BlockSpec((1, LANES), lambda b, g, j: (0, 0))
    pad16 = jnp.zeros((1, LANES), F32)
    alog = pad16.at[0, B_HEADS:2 * B_HEADS].set(a_log)
    dtb = pad16.at[0, B_HEADS:2 * B_HEADS].set(dt_bias)
    return pl.pallas_call(
        functools.partial(_gdn_kernel, tt=tt, hb=hb),
        grid=(batch, B_HEADS // hb, n_t),
        in_specs=[tok(OFF_BQ), tok(OFF_BK), tok(OFF_BV), tok(OFF_BZ),
                  pl.BlockSpec((tt, LANES), lambda b, g, j: (b * n_t + j, 0)),
                  cw(0), cw(B_QK), cw(2 * B_QK), vec, vec, vec],
        out_specs=pl.BlockSpec((tt, width), lambda b, g, j: (b * n_t + j, g)),
        out_shape=jax.ShapeDtypeStruct((t, B_V), BF16),
        scratch_shapes=[pltpu.VMEM((tt + 2 * SUBLANES, width), F32),
                        pltpu.VMEM((tt + 2 * SUBLANES, width), F32),
                        pltpu.VMEM((tt + 2 * SUBLANES, width), F32),
                        pltpu.VMEM((hb, B_DK, B_DV), F32),
                        pltpu.VMEM((LANES, tt), F32)],
        compiler_params=_params(("parallel", "parallel", "arbitrary")),
        name="gated_deltanet",
    )(proj, proj, proj, proj, ba, conv_w, conv_w, conv_w, alog, dtb, g_out.reshape(1, B_DV))


def _pack_w_in(w):
    off_b = OFF_MQ
    main = jnp.concatenate([w[:, :off_b], w[:, off_b + 2 * B_HEADS:]], axis=1).astype(BF16)
    w_ba = jnp.pad(w[:, off_b:off_b + 2 * B_HEADS], ((0, 0), (0, LANES - 2 * B_HEADS))).astype(BF16)
    return main, w_ba


def kernel(x, mem, rel_bias, g_mix, w_in, conv_w, a_log, dt_bias, g_dn_out, sinks, g_mem, w_mem_kv,
           w_br_a, w_br_b, w_br_m, w_o, g_ffn, w_ffn_in, w_ffn_out, g_final):
    batch, seq, d = x.shape
    mem_len = mem.shape[1]
    depth = w_in.shape[0]
    d_ff = w_ffn_out.shape[1]
    ff_tile = 1024
    d_ff_pad = -(-d_ff // ff_tile) * ff_tile
    off_ga = OFF_MQ + M_Q
    off_gb = off_ga + d
    off_gm = off_gb + d

    xt = x.reshape(batch * seq, d)
    memt = mem.reshape(batch * mem_len, d)
    bias = _bias_table(rel_bias)
    for l in range(depth):
        w_main, w_ba = _pack_w_in(w_in[l])
        h, ba = _rmsnorm_ba(xt, g_mix[l], w_ba)
        proj = _matmul(h, w_main, BF16)

        o_a = _swa_attention(proj, sinks[l], bias, seq)
        o_b = _gated_deltanet(proj, ba, conv_w[l], a_log[l], dt_bias[l], g_dn_out[l], batch, seq)
        mem_kv = _matmul(_rmsnorm(memt, g_mem[l], BF16), w_mem_kv[l].astype(BF16), BF16, bn=512)
        o_m = _cross_attention(proj, mem_kv, seq, mem_len)

        y = _merge(o_a, o_b, o_m, w_br_a[l].astype(BF16), w_br_b[l].astype(BF16), w_br_m[l].astype(BF16),
                   proj, off_ga, off_gb, off_gm)
        xt = _matmul_residual(y, w_o[l].astype(BF16), xt)

        w1 = w_ffn_in[l]
        w1 = jnp.concatenate([jnp.pad(w1[:, :d_ff], ((0, 0), (0, d_ff_pad - d_ff))),
                              jnp.pad(w1[:, d_ff:], ((0, 0), (0, d_ff_pad - d_ff)))], axis=1).astype(BF16)
        w2 = jnp.pad(w_ffn_out[l], ((0, d_ff_pad - d_ff), (0, 0))).astype(BF16)
        act = _swiglu_in(_rmsnorm(xt, g_ffn[l], BF16), w1, d_ff_pad)
        xt = _matmul_residual_ksplit(act, w2, xt, bk=d_ff_pad // 4)
    out = _rmsnorm(xt, g_final, F32)
    return out.reshape(batch, seq, d)
```

```python
import functools
import math

import numpy as np
import jax
import jax.numpy as jnp
from jax import lax
from jax.experimental import pallas as pl
from jax.experimental.pallas import tpu as pltpu

F32 = jnp.float32
BF16 = jnp.bfloat16

EPS = 1e-6
NEG = -1e30

A_HEADS = 32
A_KV_HEADS = 4
A_HEAD_DIM = 64
A_GROUP = A_HEADS // A_KV_HEADS
WINDOW = 128
BLOCK = 128
N_BUCKETS = 32
MAX_DISTANCE = 128
B_HEADS = 16
B_DK = 128
B_DV = 128
CONV_W = 4
CHUNK = 64
M_HEADS = 4
M_HEAD_DIM = 128

A_Q = A_HEADS * A_HEAD_DIM
A_KV = A_KV_HEADS * A_HEAD_DIM
B_QK = B_HEADS * B_DK
B_V = B_HEADS * B_DV
M_Q = M_HEADS * M_HEAD_DIM

LANES = 128
SUBLANES = 8
VMEM_LIMIT = 56 * 1024 * 1024

OFF_AQ = 0
OFF_AK = OFF_AQ + A_Q
OFF_AV = OFF_AK + A_KV
OFF_BQ = OFF_AV + A_KV
OFF_BK = OFF_BQ + B_QK
OFF_BV = OFF_BK + B_QK
OFF_BZ = OFF_BV + B_V
OFF_MQ = OFF_BZ + B_V


def _params(sem):
    return pltpu.CompilerParams(dimension_semantics=sem, vmem_limit_bytes=VMEM_LIMIT)


def _sigmoid(x):
    return 1.0 / (1.0 + jnp.exp(-x))


def _mm(a, b):
    return jnp.dot(a.astype(BF16), b.astype(BF16), preferred_element_type=F32)


def _mm_nt(a, b):
    return lax.dot_general(a.astype(BF16), b.astype(BF16), (((1,), (1,)), ((), ())),
                           preferred_element_type=F32)


def _mm_tn(a, b):
    return lax.dot_general(a.astype(BF16), b.astype(BF16), (((0,), (0,)), ((), ())),
                           preferred_element_type=F32)


def _rmsnorm_kernel(x_ref, g_ref, o_ref):
    x = x_ref[...]
    r = lax.rsqrt(jnp.mean(x * x, axis=-1, keepdims=True) + EPS)
    o_ref[...] = ((x * r) * g_ref[...]).astype(o_ref.dtype)


def _rmsnorm(x, g, out_dtype, rows=256):
    t, d = x.shape
    rows = min(rows, t)
    return pl.pallas_call(
        _rmsnorm_kernel,
        grid=(t // rows,),
        in_specs=[pl.BlockSpec((rows, d), lambda i: (i, 0)),
                  pl.BlockSpec((1, d), lambda i: (0, 0))],
        out_specs=pl.BlockSpec((rows, d), lambda i: (i, 0)),
        out_shape=jax.ShapeDtypeStruct((t, d), out_dtype),
        compiler_params=_params(("parallel",)),
        name="rmsnorm",
    )(x, g.reshape(1, d))


def _rmsnorm_ba_kernel(x_ref, g_ref, wba_ref, h_ref, ba_ref):
    x = x_ref[...]
    r = lax.rsqrt(jnp.mean(x * x, axis=-1, keepdims=True) + EPS)
    h = ((x * r) * g_ref[...]).astype(BF16)
    h_ref[...] = h
    ba_ref[...] = jnp.dot(h, wba_ref[...], preferred_element_type=F32)


def _rmsnorm_ba(x, g, w_ba, rows=256):
    t, d = x.shape
    return pl.pallas_call(
        _rmsnorm_ba_kernel,
        grid=(t // rows,),
        in_specs=[pl.BlockSpec((rows, d), lambda i: (i, 0)),
                  pl.BlockSpec((1, d), lambda i: (0, 0)),
                  pl.BlockSpec((d, LANES), lambda i: (0, 0))],
        out_specs=[pl.BlockSpec((rows, d), lambda i: (i, 0)),
                   pl.BlockSpec((rows, LANES), lambda i: (i, 0))],
        out_shape=[jax.ShapeDtypeStruct((t, d), BF16),
                   jax.ShapeDtypeStruct((t, LANES), F32)],
        compiler_params=_params(("parallel",)),
        name="rmsnorm_ba",
    )(x, g.reshape(1, d), w_ba)


def _mm_kernel(x_ref, w_ref, o_ref):
    o_ref[...] = jnp.dot(x_ref[...], w_ref[...], preferred_element_type=F32).astype(o_ref.dtype)


def _matmul(x, w, out_dtype, bm=1024, bn=1024):
    m, k = x.shape
    n = w.shape[1]
    bm = min(bm, m)
    return pl.pallas_call(
        _mm_kernel,
        grid=(m // bm, n // bn),
        in_specs=[pl.BlockSpec((bm, k), lambda i, j: (i, 0)),
                  pl.BlockSpec((k, bn), lambda i, j: (0, j))],
        out_specs=pl.BlockSpec((bm, bn), lambda i, j: (i, j)),
        out_shape=jax.ShapeDtypeStruct((m, n), out_dtype),
        compiler_params=_params(("parallel", "arbitrary")),
        name="matmul",
    )(x, w)


def _mm_res_kernel(x_ref, w_ref, res_ref, o_ref):
    o_ref[...] = res_ref[...] + jnp.dot(x_ref[...], w_ref[...], preferred_element_type=F32)


def _matmul_residual(x, w, res, bm=1024, bn=1024):
    m, k = x.shape
    n = w.shape[1]
    bm = min(bm, m)
    return pl.pallas_call(
        _mm_res_kernel,
        grid=(m // bm, n // bn),
        in_specs=[pl.BlockSpec((bm, k), lambda i, j: (i, 0)),
                  pl.BlockSpec((k, bn), lambda i, j: (0, j)),
                  pl.BlockSpec((bm, bn), lambda i, j: (i, j))],
        out_specs=pl.BlockSpec((bm, bn), lambda i, j: (i, j)),
        out_shape=jax.ShapeDtypeStruct((m, n), F32),
        compiler_params=_params(("parallel", "arbitrary")),
        name="matmul_residual",
    )(x, w, res)


def _mm_res_acc_kernel(x_ref, w_ref, res_ref, o_ref, acc_ref, *, nk):
    kk = pl.program_id(2)

    @pl.when(kk == 0)
    def _():
        acc_ref[...] = jnp.zeros_like(acc_ref)

    acc_ref[...] += jnp.dot(x_ref[...], w_ref[...], preferred_element_type=F32)

    @pl.when(kk == nk - 1)
    def _():
        o_ref[...] = res_ref[...] + acc_ref[...]


def _matmul_residual_ksplit(x, w, res, bk, bm=1024, bn=1024):
    m, k = x.shape
    n = w.shape[1]
    bm = min(bm, m)
    nk = k // bk
    return pl.pallas_call(
        functools.partial(_mm_res_acc_kernel, nk=nk),
        grid=(m // bm, n // bn, nk),
        in_specs=[pl.BlockSpec((bm, bk), lambda i, j, l: (i, l)),
                  pl.BlockSpec((bk, bn), lambda i, j, l: (l, j)),
                  pl.BlockSpec((bm, bn), lambda i, j, l: (i, j))],
        out_specs=pl.BlockSpec((bm, bn), lambda i, j, l: (i, j)),
        out_shape=jax.ShapeDtypeStruct((m, n), F32),
        scratch_shapes=[pltpu.VMEM((bm, bn), F32)],
        compiler_params=_params(("parallel", "parallel", "arbitrary")),
        name="matmul_residual_ksplit",
    )(x, w, res)


def _swiglu_in_kernel(x_ref, wg_ref, wu_ref, o_ref):
    x = x_ref[...]
    g = jnp.dot(x, wg_ref[...], preferred_element_type=F32)
    u = jnp.dot(x, wu_ref[...], preferred_element_type=F32)
    o_ref[...] = ((g * _sigmoid(g)) * u).astype(o_ref.dtype)


def _swiglu_in(x, w, n_ff, bm=1024, bn=512):
    m, k = x.shape
    bm = min(bm, m)
    nb = n_ff // bn
    return pl.pallas_call(
        _swiglu_in_kernel,
        grid=(m // bm, nb),
        in_specs=[pl.BlockSpec((bm, k), lambda i, j: (i, 0)),
                  pl.BlockSpec((k, bn), lambda i, j: (0, j)),
                  pl.BlockSpec((k, bn), lambda i, j: (0, j + nb))],
        out_specs=pl.BlockSpec((bm, bn), lambda i, j: (i, j)),
        out_shape=jax.ShapeDtypeStruct((m, n_ff), BF16),
        compiler_params=_params(("parallel", "arbitrary")),
        name="swiglu_in",
    )(x, w, w)


def _merge_kernel(oa_ref, ob_ref, om_ref, wa_ref, wb_ref, wm_ref, ga_ref, gb_ref, gm_ref, y_ref):
    ya = jnp.dot(oa_ref[...], wa_ref[...], preferred_element_type=F32)
    y = _sigmoid(ga_ref[...].astype(F32)) * ya
    yb = jnp.dot(ob_ref[...], wb_ref[...], preferred_element_type=F32)
    y = y + _sigmoid(gb_ref[...].astype(F32)) * yb
    ym = jnp.dot(om_ref[...], wm_ref[...], preferred_element_type=F32)
    y = y + _sigmoid(gm_ref[...].astype(F32)) * ym
    y_ref[...] = y.astype(y_ref.dtype)


def _merge(o_a, o_b, o_m, w_a, w_b, w_m, proj, off_ga, off_gb, off_gm, bm=1024, bn=512):
    m = o_a.shape[0]
    n = w_a.shape[1]
    bm = min(bm, m)
    ja, jb, jm = off_ga // bn, off_gb // bn, off_gm // bn

    def row(width):
        return pl.BlockSpec((bm, width), lambda i, j: (i, 0))

    def col(height):
        return pl.BlockSpec((height, bn), lambda i, j: (0, j))

    def gate(j0):
        return pl.BlockSpec((bm, bn), lambda i, j: (i, j0 + j))

    return pl.pallas_call(
        _merge_kernel,
        grid=(m // bm, n // bn),
        in_specs=[row(o_a.shape[1]), row(o_b.shape[1]), row(o_m.shape[1]),
                  col(w_a.shape[0]), col(w_b.shape[0]), col(w_m.shape[0]),
                  gate(ja), gate(jb), gate(jm)],
        out_specs=pl.BlockSpec((bm, bn), lambda i, j: (i, j)),
        out_shape=jax.ShapeDtypeStruct((m, n), BF16),
        compiler_params=_params(("parallel", "arbitrary")),
        name="merge",
    )(o_a, o_b, o_m, w_a, w_b, w_m, proj, proj, proj)


def _t5_bucket_table():
    qi = np.arange(BLOCK)[:, None]
    kj = np.arange(2 * BLOCK)[None, :]
    dist = qi + BLOCK - kj
    band = (dist >= 0) & (dist < WINDOW)
    n = np.maximum(dist, 0)
    max_exact = N_BUCKETS // 2
    nf = np.maximum(n, 1).astype(np.float32)
    large = max_exact + (np.log(nf / np.float32(max_exact)) / np.float32(math.log(MAX_DISTANCE / max_exact))
                         * np.float32(N_BUCKETS - max_exact)).astype(np.int32)
    large = np.minimum(large, N_BUCKETS - 1)
    bucket = np.where(n < max_exact, n, large)
    return np.where(band, bucket, -1).astype(np.int32)


def _bias_table_kernel(rel_ref, idx_ref, o_ref):
    h = pl.program_id(0)
    idx = idx_ref[...]
    acc = jnp.full(idx.shape, NEG, F32)
    for b in range(N_BUCKETS):
        acc = jnp.where(idx == b, rel_ref[b, h], acc)
    o_ref[0] = acc


def _bias_table(rel_bias):
    idx = jnp.asarray(_t5_bucket_table())
    return pl.pallas_call(
        _bias_table_kernel,
        grid=(A_HEADS,),
        in_specs=[pl.BlockSpec(memory_space=pltpu.SMEM),
                  pl.BlockSpec((BLOCK, 2 * BLOCK), lambda h: (0, 0))],
        out_specs=pl.BlockSpec((1, BLOCK, 2 * BLOCK), lambda h: (h, 0, 0)),
        out_shape=jax.ShapeDtypeStruct((A_HEADS, BLOCK, 2 * BLOCK), F32),
        compiler_params=_params(("arbitrary",)),
        name="swa_bias_table",
    )(rel_bias, idx)


def _swa_kernel(sink_ref, q_ref, kc_ref, kp_ref, vc_ref, vp_ref, bias_ref, o_ref, *, nb):
    t = pl.program_id(0)
    first = (t % nb) == 0
    col = lax.broadcasted_iota(jnp.int32, (BLOCK, 2 * BLOCK), 1)
    missing = jnp.logical_and(first, col < BLOCK)
    scale = A_HEAD_DIM ** -0.5
    for kh in range(A_KV_HEADS):
        ks = slice(kh * A_HEAD_DIM, (kh + 1) * A_HEAD_DIM)
        kw = jnp.concatenate([kp_ref[:, ks], kc_ref[:, ks]], axis=0)
        vw = jnp.concatenate([vp_ref[:, ks], vc_ref[:, ks]], axis=0)
        for g in range(A_GROUP):
            h = kh * A_GROUP + g
            hs = slice(h * A_HEAD_DIM, (h + 1) * A_HEAD_DIM)
            s = _mm_nt(q_ref[:, hs], kw) * scale + bias_ref[h]
            s = jnp.where(missing, NEG, s)
            sk = sink_ref[h]
            m = jnp.maximum(jnp.max(s, axis=-1, keepdims=True), sk)
            p = jnp.exp(s - m)
            denom = jnp.sum(p, axis=-1, keepdims=True) + jnp.exp(sk - m)
            o = _mm(p, vw) * (1.0 / denom)
            o_ref[:, hs] = o.astype(o_ref.dtype)


def _swa_attention(proj, sinks, bias, seq):
    t = proj.shape[0]
    nb = seq // BLOCK
    jk, jv = OFF_AK // A_KV, OFF_AV // A_KV

    def cur(jc):
        return pl.BlockSpec((BLOCK, A_KV), lambda i: (i, jc))

    def prev(jc):
        return pl.BlockSpec((BLOCK, A_KV), lambda i: (jnp.maximum(i - 1, 0), jc))

    return pl.pallas_call(
        functools.partial(_swa_kernel, nb=nb),
        grid=(t // BLOCK,),
        in_specs=[pl.BlockSpec(memory_space=pltpu.SMEM),
                  pl.BlockSpec((BLOCK, A_Q), lambda i: (i, OFF_AQ // A_Q)),
                  cur(jk), prev(jk), cur(jv), prev(jv),
                  pl.BlockSpec((A_HEADS, BLOCK, 2 * BLOCK), lambda i: (0, 0, 0))],
        out_specs=pl.BlockSpec((BLOCK, A_Q), lambda i: (i, 0)),
        out_shape=jax.ShapeDtypeStruct((t, A_Q), BF16),
        compiler_params=_params(("parallel",)),
        name="swa_attention",
    )(sinks, proj, proj, proj, proj, proj, bias)


def _xattn_kernel(q_ref, kv_ref, o_ref):
    scale = M_HEAD_DIM ** -0.5
    for h in range(M_HEADS):
        hs = slice(h * M_HEAD_DIM, (h + 1) * M_HEAD_DIM)
        vs = slice(M_Q + h * M_HEAD_DIM, M_Q + (h + 1) * M_HEAD_DIM)
        s = _mm_nt(q_ref[:, hs], kv_ref[:, hs]) * scale
        m = jnp.max(s, axis=-1, keepdims=True)
        p = jnp.exp(s - m)
        denom = jnp.sum(p, axis=-1, keepdims=True)
        o = _mm(p, kv_ref[:, vs]) * (1.0 / denom)
        o_ref[:, hs] = o.astype(o_ref.dtype)


def _cross_attention(proj, mem_kv, seq, mem_len, rows=512):
    t = proj.shape[0]
    rows = min(rows, seq)
    per_batch = seq // rows
    return pl.pallas_call(
        _xattn_kernel,
        grid=(t // rows,),
        in_specs=[pl.BlockSpec((rows, M_Q), lambda i: (i, OFF_MQ // M_Q)),
                  pl.BlockSpec((mem_len, 2 * M_Q), lambda i: (i // per_batch, 0))],
        out_specs=pl.BlockSpec((rows, M_Q), lambda i: (i, 0)),
        out_shape=jax.ShapeDtypeStruct((t, M_Q), BF16),
        compiler_params=_params(("parallel",)),
        name="cross_attention",
    )(proj, mem_kv)


def _unit_lower_inverse_minus_eye(a_mats, eye, blk16, blk32):
    assert CHUNK == 64
    nd = [jnp.where(blk16, -a, 0.0) for a in a_mats]
    p = [eye + n for n in nd]
    pw = [_mm(n, n) for n in nd]
    for _ in range(2):
        r = [_mm(jnp.concatenate([w, q], axis=0), w) for w, q in zip(pw, p)]
        pw = [x[:CHUNK] for x in r]
        p = [q + x[CHUNK:] for q, x in zip(p, r)]
    d = [q + _mm(q, w) for q, w in zip(p, pw)]
    for inner, outer in ((blk16, blk32), (blk32, None)):
        if outer is None:
            c_off = [jnp.where(inner, 0.0, a) for a in a_mats]
        else:
            c_off = [jnp.where(jnp.logical_and(outer, jnp.logical_not(inner)), a, 0.0) for a in a_mats]
        x = [_mm(dd, c) for dd, c in zip(d, c_off)]
        d = [dd - _mm(xx, dd) for dd, xx in zip(d, x)]
    return [dd - eye for dd in d]


def _gdn_kernel(q_ref, k_ref, v_ref, z_ref, ba_ref, cwq_ref, cwk_ref, cwv_ref, alog_ref, dtb_ref, gout_ref,
                o_ref, xq_ref, xk_ref, xv_ref, s_ref, gt_ref, *, tt, hb):
    hg = pl.program_id(1)
    j = pl.program_id(2)
    pad = SUBLANES
    nc = tt // CHUNK

    @pl.when(j == 0)
    def _():
        for r in (xq_ref, xk_ref, xv_ref):
            r[0:pad, :] = jnp.zeros((pad, r.shape[1]), F32)
        s_ref[...] = jnp.zeros_like(s_ref)

    @pl.when(j > 0)
    def _():
        for r in (xq_ref, xk_ref, xv_ref):
            r[0:pad, :] = r[tt:tt + pad, :]

    def conv_silu(x_ref, xe_ref, cw_ref):
        xe_ref[pad:pad + tt, :] = x_ref[...].astype(F32)
        w = cw_ref[...]
        y = xe_ref[pad - 3:pad - 3 + tt, :] * w[0:1, :]
        for i in range(1, CONV_W):
            y = y + xe_ref[pad - 3 + i:pad - 3 + i + tt, :] * w[i:i + 1, :]
        return y * _sigmoid(y)

    qs = conv_silu(q_ref, xq_ref, cwq_ref)
    ks = conv_silu(k_ref, xk_ref, cwk_ref)
    vs = conv_silu(v_ref, xv_ref, cwv_ref)

    ba = ba_ref[...]
    lane = lax.broadcasted_iota(jnp.int32, (tt, LANES), 1)
    pos = lax.broadcasted_iota(jnp.int32, (tt, LANES), 0) % CHUNK
    beta_all = _sigmoid(ba)
    xg = ba + dtb_ref[...]
    softplus = jnp.maximum(xg, 0.0) + jnp.log1p(jnp.exp(-jnp.abs(xg)))
    gc_all = -jnp.exp(alog_ref[...]) * softplus
    step = 1
    while step < CHUNK:
        gc_all = gc_all + jnp.where(pos >= step, pltpu.roll(gc_all, step, axis=0), 0.0)
        step *= 2
    gt_ref[...] = gc_all.T

    row = lax.broadcasted_iota(jnp.int32, (CHUNK, CHUNK), 0)
    colm = lax.broadcasted_iota(jnp.int32, (CHUNK, CHUNK), 1)
    tri = row >= colm
    strict = row > colm
    blk16 = (row // 16) == (colm // 16)
    blk32 = (row // 32) == (colm // 32)
    eye = jnp.where(row == colm, 1.0, 0.0).astype(F32)

    def pick(arr, l):
        return jnp.sum(jnp.where(lane == l, arr, 0.0), axis=1, keepdims=True)

    gw = gout_ref[...]
    heads = []
    for i in range(hb):
        h = hg * hb + i
        hs = slice(i * B_DK, (i + 1) * B_DK)
        q = qs[:, hs]
        k = ks[:, hs]
        q = (q * lax.rsqrt(jnp.sum(q * q, axis=-1, keepdims=True) + EPS)) * (B_DK ** -0.5)
        k = k * lax.rsqrt(jnp.sum(k * k, axis=-1, keepdims=True) + EPS)
        beta = pick(beta_all, h)
        gcc = pick(gc_all, B_HEADS + h)
        gcr = gt_ref[pl.ds(B_HEADS + h, 1), :]
        egc = jnp.exp(gcc)
        kb = k * beta
        zf = z_ref[:, hs].astype(F32)
        heads.append(dict(q=q, k=k, kt=k.T, kb=kb, gcc=gcc, gcr=gcr, qe=q * egc,
                          rhs=jnp.concatenate([vs[:, hs] * beta, kb * egc], axis=1),
                          zg=zf * _sigmoid(zf), hs=hs))

    probs = [(i, c) for c in range(nc) for i in range(hb)]
    rsl = [slice(c * CHUNK, (c + 1) * CHUNK) for c in range(nc)]
    decay = [jnp.exp(jnp.where(tri, heads[i]["gcc"][rsl[c]] - heads[i]["gcr"][:, rsl[c]], NEG)) for i, c in probs]
    kq = [_mm_nt(jnp.concatenate([heads[i]["kb"][rsl[c]], heads[i]["q"][rsl[c]]], axis=0), heads[i]["k"][rsl[c]])
          for i, c in probs]
    a_mats = [jnp.where(strict, x[:CHUNK] * dc, 0.0) for x, dc in zip(kq, decay)]
    att = [x[CHUNK:] * dc for x, dc in zip(kq, decay)]
    t_minus_i = _unit_lower_inverse_minus_eye(a_mats, eye, blk16, blk32)
    rhs = [heads[i]["rhs"][rsl[c]] for i, c in probs]
    sol = [r + _mm(tm, r) for tm, r in zip(t_minus_i, rhs)]

    state = [s_ref[i] for i in range(hb)]
    for c in range(nc):
        rs = rsl[c]
        idx = [c * hb + i for i in range(hb)]
        ws = [_mm(jnp.concatenate([sol[p][:, B_DV:], heads[i]["qe"][rs]], axis=0), state[i])
              for i, p in enumerate(idx)]
        v_new = [sol[p][:, :B_DV] - w[:CHUNK] for p, w in zip(idx, ws)]
        g_last = [heads[i]["gcc"][rs][CHUNK - 1:CHUNK, :] for i in range(hb)]
        lhs = [jnp.concatenate([att[p], heads[i]["kt"][:, rs] * jnp.exp(g_last[i] - heads[i]["gcr"][:, rs])], axis=0)
               for i, p in enumerate(idx)]
        r2 = [_mm(l, v) for l, v in zip(lhs, v_new)]
        for i in range(hb):
            state[i] = state[i] * jnp.exp(g_last[i]) + r2[i][CHUNK:]
            o = ws[i][CHUNK:] + r2[i][:CHUNK]
            o = (o * lax.rsqrt(jnp.mean(o * o, axis=-1, keepdims=True) + EPS)) * gw
            o_ref[rs, heads[i]["hs"]] = (o * heads[i]["zg"][rs]).astype(o_ref.dtype)
    for i in range(hb):
        s_ref[i] = state[i]


def _gated_deltanet(proj, ba, conv_w, a_log, dt_bias, g_out, batch, seq, tt=256, hb=4):
    t = proj.shape[0]
    tt = min(tt, seq)
    width = hb * B_DK
    n_t = seq // tt

    def tok(off):
        j0 = off // width
        return pl.BlockSpec((tt, width), lambda b, g, j: (b * n_t + j, j0 + g))

    def cw(off):
        j0 = off // width
        return pl.BlockSpec((CONV_W, width), lambda b, g, j: (0, j0 + g))

    vec = pl.BlockSpec((1, LANES), lambda b, g, j: (0, 0))
    pad16 = jnp.zeros((1, LANES), F32)
    alog = pad16.at[0, B_HEADS:2 * B_HEADS].set(a_log)
    dtb = pad16.at[0, B_HEADS:2 * B_HEADS].set(dt_bias)
    return pl.pallas_call(
        functools.partial(_gdn_kernel, tt=tt, hb=hb),
        grid=(batch, B_HEADS // hb, n_t),
        in_specs=[tok(OFF_BQ), tok(OFF_BK), tok(OFF_BV), tok(OFF_BZ),
                  pl.BlockSpec((tt, LANES), lambda b, g, j: (b * n_t + j, 0)),
                  cw(0), cw(B_QK), cw(2 * B_QK), vec, vec, vec],
        out_specs=pl.BlockSpec((tt, width), lambda b, g, j: (b * n_t + j, g)),
        out_shape=jax.ShapeDtypeStruct((t, B_V), BF16),
        scratch_shapes=[pltpu.VMEM((tt + 2 * SUBLANES, width), F32),
                        pltpu.VMEM((tt + 2 * SUBLANES, width), F32),
                        pltpu.VMEM((tt + 2 * SUBLANES, width), F32),
                        pltpu.VMEM((hb, B_DK, B_DV), F32),
                        pltpu.VMEM((LANES, tt), F32)],
        compiler_params=_params(("parallel", "parallel", "arbitrary")),
        name="gated_deltanet",
    )(proj, proj, proj, proj, ba, conv_w, conv_w, conv_w, alog, dtb, g_out.reshape(1, B_DV))


def _pack_w_in(w):
    off_b = OFF_MQ
    main = jnp.concatenate([w[:, :off_b], w[:, off_b + 2 * B_HEADS:]], axis=1).astype(BF16)
    w_ba = jnp.pad(w[:, off_b:off_b + 2 * B_HEADS], ((0, 0), (0, LANES - 2 * B_HEADS))).astype(BF16)
    return main, w_ba


def kernel(x, mem, rel_bias, g_mix, w_in, conv_w, a_log, dt_bias, g_dn_out, sinks, g_mem, w_mem_kv,
           w_br_a, w_br_b, w_br_m, w_o, g_ffn, w_ffn_in, w_ffn_out, g_final):
    batch, seq, d = x.shape
    mem_len = mem.shape[1]
    depth = w_in.shape[0]
    d_ff = w_ffn_out.shape[1]
    ff_tile = 1024
    d_ff_pad = -(-d_ff // ff_tile) * ff_tile
    off_ga = OFF_MQ + M_Q
    off_gb = off_ga + d
    off_gm = off_gb + d

    xt = x.reshape(batch * seq, d)
    memt = mem.reshape(batch * mem_len, d)
    bias = _bias_table(rel_bias)
    for l in range(depth):
        w_main, w_ba = _pack_w_in(w_in[l])
        h, ba = _rmsnorm_ba(xt, g_mix[l], w_ba)
        proj = _matmul(h, w_main, BF16)

        o_a = _swa_attention(proj, sinks[l], bias, seq)
        o_b = _gated_deltanet(proj, ba, conv_w[l], a_log[l], dt_bias[l], g_dn_out[l], batch, seq)
        mem_kv = _matmul(_rmsnorm(memt, g_mem[l], BF16), w_mem_kv[l].astype(BF16), BF16, bn=512)
        o_m = _cross_attention(proj, mem_kv, seq, mem_len)

        y = _merge(o_a, o_b, o_m, w_br_a[l].astype(BF16), w_br_b[l].astype(BF16), w_br_m[l].astype(BF16),
                   proj, off_ga, off_gb, off_gm)
        xt = _matmul_residual(y, w_o[l].astype(BF16), xt)

        w1 = w_ffn_in[l]
        w1 = jnp.concatenate([jnp.pad(w1[:, :d_ff], ((0, 0), (0, d_ff_pad - d_ff))),
                              jnp.pad(w1[:, d_ff:], ((0, 0), (0, d_ff_pad - d_ff)))], axis=1).astype(BF16)
        w2 = jnp.pad(w_ffn_out[l], ((0, d_ff_pad - d_ff), (0, 0))).astype(BF16)
        act = _swiglu_in(_rmsnorm(xt, g_ffn[l], BF16), w1, d_ff_pad)
        xt = _matmul_residual_ksplit(act, w2, xt, bk=d_ff_pad // 4)
    out = _rmsnorm(xt, g_final, F32)
    return out.reshape(batch, seq, d)
```

```python
import functools
import math

import numpy as np
import jax
import jax.numpy as jnp
from jax import lax
from jax.experimental import pallas as pl
from jax.experimental.pallas import tpu as pltpu

F32 = jnp.float32
BF16 = jnp.bfloat16

EPS = 1e-6
NEG = -1e30

A_HEADS = 32
A_KV_HEADS = 4
A_HEAD_DIM = 64
A_GROUP = A_HEADS // A_KV_HEADS
WINDOW = 128
BLOCK = 128
N_BUCKETS = 32
MAX_DISTANCE = 128
B_HEADS = 16
B_DK = 128
B_DV = 128
CONV_W = 4
CHUNK = 64
M_HEADS = 4
M_HEAD_DIM = 128

A_Q = A_HEADS * A_HEAD_DIM
A_KV = A_KV_HEADS * A_HEAD_DIM
B_QK = B_HEADS * B_DK
B_V = B_HEADS * B_DV
M_Q = M_HEADS * M_HEAD_DIM

LANES = 128
SUBLANES = 8
VMEM_LIMIT = 56 * 1024 * 1024

OFF_AQ = 0
OFF_AK = OFF_AQ + A_Q
OFF_AV = OFF_AK + A_KV
OFF_BQ = OFF_AV + A_KV
OFF_BK = OFF_BQ + B_QK
OFF_BV = OFF_BK + B_QK
OFF_BZ = OFF_BV + B_V
OFF_MQ = OFF_BZ + B_V

PROJ_TOK_BN = 1536
PROJ_GATE_BN = 1280


def _params(sem):
    return pltpu.CompilerParams(dimension_semantics=sem, vmem_limit_bytes=VMEM_LIMIT)


def _sigmoid(x):
    return 0.5 * jnp.tanh(0.5 * x) + 0.5


def _mm(a, b):
    return jnp.dot(a.astype(BF16), b.astype(BF16), preferred_element_type=F32)


def _mm_nt(a, b):
    return lax.dot_general(a.astype(BF16), b.astype(BF16), (((1,), (1,)), ((), ())),
                           preferred_element_type=F32)


def _mm_tn(a, b):
    return lax.dot_general(a.astype(BF16), b.astype(BF16), (((0,), (0,)), ((), ())),
                           preferred_element_type=F32)


def _rmsnorm_kernel(x_ref, g_ref, o_ref):
    x = x_ref[...]
    r = lax.rsqrt(jnp.mean(x * x, axis=-1, keepdims=True) + EPS)
    o_ref[...] = ((x * r) * g_ref[...]).astype(o_ref.dtype)


def _rmsnorm(x, g, out_dtype, rows=256):
    t, d = x.shape
    rows = min(rows, t)
    return pl.pallas_call(
        _rmsnorm_kernel,
        grid=(t // rows,),
        in_specs=[pl.BlockSpec((rows, d), lambda i: (i, 0)),
                  pl.BlockSpec((1, d), lambda i: (0, 0))],
        out_specs=pl.BlockSpec((rows, d), lambda i: (i, 0)),
        out_shape=jax.ShapeDtypeStruct((t, d), out_dtype),
        compiler_params=_params(("parallel",)),
        name="rmsnorm",
    )(x, g.reshape(1, d))


def _rmsnorm_ba_kernel(x_ref, g_ref, wba_ref, h_ref, ba_ref):
    x = x_ref[...]
    r = lax.rsqrt(jnp.mean(x * x, axis=-1, keepdims=True) + EPS)
    h = ((x * r) * g_ref[...]).astype(BF16)
    h_ref[...] = h
    ba_ref[...] = _mm_nt(h, wba_ref[...])


def _rmsnorm_ba(x, g, wt, ba_row, rows=256):
    t, d = x.shape
    assert ba_row % LANES == 0
    return pl.pallas_call(
        _rmsnorm_ba_kernel,
        grid=(t // rows,),
        in_specs=[pl.BlockSpec((rows, d), lambda i: (i, 0)),
                  pl.BlockSpec((1, d), lambda i: (0, 0)),
                  pl.BlockSpec((LANES, d), lambda i: (ba_row // LANES, 0))],
        out_specs=[pl.BlockSpec((rows, d), lambda i: (i, 0)),
                   pl.BlockSpec((rows, LANES), lambda i: (i, 0))],
        out_shape=[jax.ShapeDtypeStruct((t, d), BF16),
                   jax.ShapeDtypeStruct((t, LANES), F32)],
        compiler_params=_params(("parallel",)),
        name="rmsnorm_ba",
    )(x, g.reshape(1, d), wt)


def _mm_kernel(x_ref, w_ref, o_ref):
    o_ref[...] = jnp.dot(x_ref[...], w_ref[...], preferred_element_type=F32).astype(o_ref.dtype)


def _matmul(x, w, out_dtype, bm=1024, bn=1024):
    m, k = x.shape
    n = w.shape[1]
    bm = min(bm, m)
    return pl.pallas_call(
        _mm_kernel,
        grid=(m // bm, n // bn),
        in_specs=[pl.BlockSpec((bm, k), lambda i, j: (i, 0)),
                  pl.BlockSpec((k, bn), lambda i, j: (0, j))],
        out_specs=pl.BlockSpec((bm, bn), lambda i, j: (i, j)),
        out_shape=jax.ShapeDtypeStruct((m, n), out_dtype),
        compiler_params=_params(("parallel", "arbitrary")),
        name="matmul",
    )(x, w)


def _mm_nt_kernel(x_ref, wt_ref, o_ref):
    o_ref[...] = _mm_nt(x_ref[...], wt_ref[...]).astype(o_ref.dtype)


def _matmul_nt(x, wt, out_dtype, bn, n=None, bm=1024):
    m, k = x.shape
    n = wt.shape[0] if n is None else n
    bm = min(bm, m)
    assert n % bn == 0
    return pl.pallas_call(
        _mm_nt_kernel,
        grid=(m // bm, n // bn),
        in_specs=[pl.BlockSpec((bm, k), lambda i, j: (i, 0)),
                  pl.BlockSpec((bn, k), lambda i, j: (j, 0))],
        out_specs=pl.BlockSpec((bm, bn), lambda i, j: (i, j)),
        out_shape=jax.ShapeDtypeStruct((m, n), out_dtype),
        compiler_params=_params(("parallel", "arbitrary")),
        name="matmul_nt",
    )(x, wt)


def _mm_res_kernel(x_ref, w_ref, res_ref, o_ref):
    o_ref[...] = res_ref[...] + jnp.dot(x_ref[...], w_ref[...], preferred_element_type=F32)


def _matmul_residual(x, w, res, bm=1024, bn=1024):
    m, k = x.shape
    n = w.shape[1]
    bm = min(bm, m)
    return pl.pallas_call(
        _mm_res_kernel,
        grid=(m // bm, n // bn),
        in_specs=[pl.BlockSpec((bm, k), lambda i, j: (i, 0)),
                  pl.BlockSpec((k, bn), lambda i, j: (0, j)),
                  pl.BlockSpec((bm, bn), lambda i, j: (i, j))],
        out_specs=pl.BlockSpec((bm, bn), lambda i, j: (i, j)),
        out_shape=jax.ShapeDtypeStruct((m, n), F32),
        compiler_params=_params(("parallel", "arbitrary")),
        name="matmul_residual",
    )(x, w, res)


def _swiglu_in_kernel(x_ref, wg_ref, wu_ref, o_ref):
    x = x_ref[...]
    g = jnp.dot(x, wg_ref[...], preferred_element_type=F32)
    u = jnp.dot(x, wu_ref[...], preferred_element_type=F32)
    o_ref[...] = ((g * _sigmoid(g)) * u).astype(o_ref.dtype)


def _swiglu_in(x, w, n_ff, bm=1024, bn=256):
    m, k = x.shape
    bm = min(bm, m)
    nb = n_ff // bn
    assert nb * bn == n_ff
    return pl.pallas_call(
        _swiglu_in_kernel,
        grid=(m // bm, nb),
        in_specs=[pl.BlockSpec((bm, k), lambda i, j: (i, 0)),
                  pl.BlockSpec((k, bn), lambda i, j: (0, j)),
                  pl.BlockSpec((k, bn), lambda i, j: (0, j + nb))],
        out_specs=pl.BlockSpec((bm, bn), lambda i, j: (i, j)),
        out_shape=jax.ShapeDtypeStruct((m, n_ff), BF16),
        compiler_params=_params(("parallel", "arbitrary")),
        name="swiglu_in",
    )(x, w, w)


def _merge_kernel(oa_ref, ob_ref, om_ref, wa_ref, wb_ref, wm_ref, ga_ref, gb_ref, gm_ref, y_ref):
    ya = jnp.dot(oa_ref[...], wa_ref[...], preferred_element_type=F32)
    y = _sigmoid(ga_ref[...].astype(F32)) * ya
    yb = jnp.dot(ob_ref[...], wb_ref[...], preferred_element_type=F32)
    y = y + _sigmoid(gb_ref[...].astype(F32)) * yb
    ym = jnp.dot(om_ref[...], wm_ref[...], preferred_element_type=F32)
    y = y + _sigmoid(gm_ref[...].astype(F32)) * ym
    y_ref[...] = y.astype(y_ref.dtype)


def _merge(o_a, o_b, o_m, w_a, w_b, w_m, proj, off_ga, off_gb, off_gm, bm=1024, bn=512):
    m = o_a.shape[0]
    n = w_a.shape[1]
    bm = min(bm, m)
    ja, jb, jm = off_ga // bn, off_gb // bn, off_gm // bn

    def row(width):
        return pl.BlockSpec((bm, width), lambda i, j: (i, 0))

    def col(height):
        return pl.BlockSpec((height, bn), lambda i, j: (0, j))

    def gate(j0):
        return pl.BlockSpec((bm, bn), lambda i, j: (i, j0 + j))

    return pl.pallas_call(
        _merge_kernel,
        grid=(m // bm, n // bn),
        in_specs=[row(o_a.shape[1]), row(o_b.shape[1]), row(o_m.shape[1]),
                  col(w_a.shape[0]), col(w_b.shape[0]), col(w_m.shape[0]),
                  gate(ja), gate(jb), gate(jm)],
        out_specs=pl.BlockSpec((bm, bn), lambda i, j: (i, j)),
        out_shape=jax.ShapeDtypeStruct((m, n), BF16),
        compiler_params=_params(("parallel", "arbitrary")),
        name="merge",
    )(o_a, o_b, o_m, w_a, w_b, w_m, proj, proj, proj)


def _t5_bucket_table():
    qi = np.arange(BLOCK)[:, None]
    kj = np.arange(2 * BLOCK)[None, :]
    dist = qi + BLOCK - kj
    band = (dist >= 0) & (dist < WINDOW)
    n = np.maximum(dist, 0)
    max_exact = N_BUCKETS // 2
    nf = np.maximum(n, 1).astype(np.float32)
    large = max_exact + (np.log(nf / np.float32(max_exact)) / np.float32(math.log(MAX_DISTANCE / max_exact))
                         * np.float32(N_BUCKETS - max_exact)).astype(np.int32)
    large = np.minimum(large, N_BUCKETS - 1)
    bucket = np.where(n < max_exact, n, large)
    return np.where(band, bucket, -1).astype(np.int32)


def _bias_table_kernel(rel_ref, idx_ref, o_ref):
    first = pl.program_id(0)
    h = pl.program_id(1)
    idx = idx_ref[...]
    acc = jnp.full(idx.shape, NEG, F32)
    for b in range(N_BUCKETS):
        acc = jnp.where(idx == b, rel_ref[b, h], acc)
    col = lax.broadcasted_iota(jnp.int32, idx.shape, 1)
    o_ref[0, 0] = jnp.where(jnp.logical_and(first == 1, col < BLOCK), NEG, acc)


def _bias_table(rel_bias):
    idx = jnp.asarray(_t5_bucket_table())
    return pl.pallas_call(
        _bias_table_kernel,
        grid=(2, A_HEADS),
        in_specs=[pl.BlockSpec(memory_space=pltpu.SMEM),
                  pl.BlockSpec((BLOCK, 2 * BLOCK), lambda f, h: (0, 0))],
        out_specs=pl.BlockSpec((1, 1, BLOCK, 2 * BLOCK), lambda f, h: (f, h, 0, 0)),
        out_shape=jax.ShapeDtypeStruct((2, A_HEADS, BLOCK, 2 * BLOCK), F32),
        compiler_params=_params(("arbitrary", "arbitrary")),
        name="swa_bias_table",
    )(rel_bias, idx)


def _swa_kernel(sink_ref, q_ref, kc_ref, kp_ref, vc_ref, vp_ref, bias_ref, o_ref):
    assert 2 * A_HEAD_DIM == LANES and A_GROUP % 2 == 0
    n_pair = A_GROUP // 2
    low = lax.broadcasted_iota(jnp.int32, (1, LANES), 1) < A_HEAD_DIM
    scale = A_HEAD_DIM ** -0.5
    ones_low = jnp.where(low, 1.0, 0.0).astype(F32) + jnp.zeros((2 * BLOCK, LANES), F32)
    ones_high = 1.0 - ones_low

    def kv_halves(cur_ref, prev_ref, kh):
        sl = slice((kh // 2) * LANES, (kh // 2 + 1) * LANES)
        w = jnp.concatenate([prev_ref[:, sl], cur_ref[:, sl]], axis=0).astype(F32)
        swapped = pltpu.roll(w, A_HEAD_DIM, axis=1)
        in_low, in_high = (w, swapped) if kh % 2 == 0 else (swapped, w)
        return jnp.where(low, in_low, 0.0), jnp.where(low, 0.0, in_high)

    scores = []
    for kh in range(A_KV_HEADS):
        qg = jnp.concatenate([q_ref[:, (kh * n_pair + m) * LANES:(kh * n_pair + m + 1) * LANES]
                              for m in range(n_pair)], axis=0)
        k_even, k_odd = kv_halves(kc_ref, kp_ref, kh)
        scores.append((_mm_nt(qg, k_even * scale), _mm_nt(qg, k_odd * scale)))

    for kh in range(A_KV_HEADS):
        v_even, v_odd = kv_halves(vc_ref, vp_ref, kh)
        probs = ([], [])
        sink_terms = []
        for m in range(n_pair):
            rows = slice(m * BLOCK, (m + 1) * BLOCK)
            terms = []
            for parity in range(2):
                h = kh * A_GROUP + 2 * m + parity
                s = scores[kh][parity][rows] + bias_ref[0, h]
                mx = jnp.maximum(jnp.max(s, axis=-1, keepdims=True), sink_ref[h])
                probs[parity].append(jnp.exp(s - mx).astype(BF16))
                terms.append(jnp.exp(sink_ref[h] - mx))
            sink_terms.append(jnp.where(low, terms[0], terms[1]))
        res = (_mm(jnp.concatenate(probs[0], axis=0), jnp.concatenate([v_even, ones_low], axis=1))
               + _mm(jnp.concatenate(probs[1], axis=0), jnp.concatenate([v_odd, ones_high], axis=1)))
        for m in range(n_pair):
            rows = slice(m * BLOCK, (m + 1) * BLOCK)
            o = res[rows, :LANES] / (res[rows, LANES:] + sink_terms[m])
            o_ref[:, (kh * n_pair + m) * LANES:(kh * n_pair + m + 1) * LANES] = o.astype(o_ref.dtype)


def _swa_attention(proj, sinks, bias, seq):
    t = proj.shape[0]
    nb = seq // BLOCK
    jk, jv = OFF_AK // A_KV, OFF_AV // A_KV

    def cur(jc):
        return pl.BlockSpec((BLOCK, A_KV), lambda i: (i, jc))

    def prev(jc):
        return pl.BlockSpec((BLOCK, A_KV), lambda i: (jnp.maximum(i - 1, 0), jc))

    return pl.pallas_call(
        _swa_kernel,
        grid=(t // BLOCK,),
        in_specs=[pl.BlockSpec(memory_space=pltpu.SMEM),
                  pl.BlockSpec((BLOCK, A_Q), lambda i: (i, OFF_AQ // A_Q)),
                  cur(jk), prev(jk), cur(jv), prev(jv),
                  pl.BlockSpec((1, A_HEADS, BLOCK, 2 * BLOCK),
                               lambda i: (jnp.where(i % nb == 0, 1, 0), 0, 0, 0))],
        out_specs=pl.BlockSpec((BLOCK, A_Q), lambda i: (i, 0)),
        out_shape=jax.ShapeDtypeStruct((t, A_Q), BF16),
        compiler_params=_params(("parallel",)),
        name="swa_attention",
    )(sinks, proj, proj, proj, proj, proj, bias)


def _xattn_kernel(q_ref, kv_ref, o_ref):
    scale = M_HEAD_DIM ** -0.5
    for h in range(M_HEADS):
        hs = slice(h * M_HEAD_DIM, (h + 1) * M_HEAD_DIM)
        vs = slice(M_Q + h * M_HEAD_DIM, M_Q + (h + 1) * M_HEAD_DIM)
        s = _mm_nt(q_ref[:, hs], kv_ref[:, hs]) * scale
        m = jnp.max(s, axis=-1, keepdims=True)
        p = jnp.exp(s - m)
        denom = jnp.sum(p, axis=-1, keepdims=True)
        o = _mm(p, kv_ref[:, vs]) * (1.0 / denom)
        o_ref[:, hs] = o.astype(o_ref.dtype)


def _cross_attention(gates, mem_kv, seq, mem_len, rows=512):
    t = gates.shape[0]
    rows = min(rows, seq)
    per_batch = seq // rows
    return pl.pallas_call(
        _xattn_kernel,
        grid=(t // rows,),
        in_specs=[pl.BlockSpec((rows, M_Q), lambda i: (i, 0)),
                  pl.BlockSpec((mem_len, 2 * M_Q), lambda i: (i // per_batch, 0))],
        out_specs=pl.BlockSpec((rows, M_Q), lambda i: (i, 0)),
        out_shape=jax.ShapeDtypeStruct((t, M_Q), BF16),
        compiler_params=_params(("parallel",)),
        name="cross_attention",
    )(gates, mem_kv)


def _unit_lower_inverse_minus_eye(a_mats, eye, blk16, blk32):
    assert CHUNK == 64
    nd = [jnp.where(blk16, -a, 0.0) for a in a_mats]
    p = [eye + n for n in nd]
    pw = [_mm(n, n) for n in nd]
    for _ in range(2):
        r = [_mm(jnp.concatenate([w, q], axis=0), w) for w, q in zip(pw, p)]
        pw = [x[:CHUNK] for x in r]
        p = [q + x[CHUNK:] for q, x in zip(p, r)]
    d = [q + _mm(q, w) for q, w in zip(p, pw)]
    for inner, outer in ((blk16, blk32), (blk32, None)):
        if outer is None:
            c_off = [jnp.where(inner, 0.0, a) for a in a_mats]
        else:
            c_off = [jnp.where(jnp.logical_and(outer, jnp.logical_not(inner)), a, 0.0) for a in a_mats]
        x = [_mm(dd, c) for dd, c in zip(d, c_off)]
        d = [dd - _mm(xx, dd) for dd, xx in zip(d, x)]
    return [dd - eye for dd in d]


def _gdn_kernel(q_ref, k_ref, v_ref, z_ref, ba_ref, cwq_ref, cwk_ref, cwv_ref, alog_ref, dtb_ref, gout_ref,
                o_ref, xq_ref, xk_ref, xv_ref, s_ref, gt_ref, *, tt, hb):
    hg = pl.program_id(1)
    j = pl.program_id(2)
    pad = SUBLANES
    nc = tt // CHUNK

    @pl.when(j == 0)
    def _():
        for r in (xq_ref, xk_ref, xv_ref):
            r[0:pad, :] = jnp.zeros((pad, r.shape[1]), F32)
        s_ref[...] = jnp.zeros_like(s_ref)

    @pl.when(j > 0)
    def _():
        for r in (xq_ref, xk_ref, xv_ref):
            r[0:pad, :] = r[tt:tt + pad, :]

    def conv_silu(x_ref, xe_ref, cw_ref):
        xe_ref[pad:pad + tt, :] = x_ref[...].astype(F32)
        w = cw_ref[...]
        xe = xe_ref[0:pad + tt, :]
        taps = [pltpu.roll(xe, CONV_W - 1 - i, axis=0)[pad:] for i in range(CONV_W - 1)] + [xe[pad:]]
        y = taps[0] * w[0:1, :]
        for i in range(1, CONV_W):
            y = y + taps[i] * w[i:i + 1, :]
        return y * _sigmoid(y)

    qs = conv_silu(q_ref, xq_ref, cwq_ref)
    ks = conv_silu(k_ref, xk_ref, cwk_ref)
    vs = conv_silu(v_ref, xv_ref, cwv_ref)

    ba = ba_ref[...]
    lane = lax.broadcasted_iota(jnp.int32, (tt, LANES), 1)
    pos = lax.broadcasted_iota(jnp.int32, (tt, LANES), 0) % CHUNK
    beta_all = _sigmoid(ba)
    xg = ba + dtb_ref[...]
    softplus = jnp.maximum(xg, 0.0) + jnp.log1p(jnp.exp(-jnp.abs(xg)))
    gc_all = -jnp.exp(alog_ref[...]) * softplus
    step = 1
    while step < CHUNK:
        gc_all = gc_all + jnp.where(pos >= step, pltpu.roll(gc_all, step, axis=0), 0.0)
        step *= 2
    gt_ref[...] = gc_all.T

    row = lax.broadcasted_iota(jnp.int32, (CHUNK, CHUNK), 0)
    colm = lax.broadcasted_iota(jnp.int32, (CHUNK, CHUNK), 1)
    tri = row >= colm
    strict = row > colm
    blk16 = (row // 16) == (colm // 16)
    blk32 = (row // 32) == (colm // 32)
    eye = jnp.where(row == colm, 1.0, 0.0).astype(F32)

    def pick(arr, l):
        return jnp.sum(jnp.where(lane == l, arr, 0.0), axis=1, keepdims=True)

    gw = gout_ref[...]
    heads = []
    for i in range(hb):
        h = hg * hb + i
        hs = slice(i * B_DK, (i + 1) * B_DK)
        q = qs[:, hs]
        k = ks[:, hs]
        q = (q * lax.rsqrt(jnp.sum(q * q, axis=-1, keepdims=True) + EPS)) * (B_DK ** -0.5)
        k = k * lax.rsqrt(jnp.sum(k * k, axis=-1, keepdims=True) + EPS)
        beta = pick(beta_all, h)
        gcc = pick(gc_all, B_HEADS + h)
        gcr = gt_ref[pl.ds(B_HEADS + h, 1), :]
        egc = jnp.exp(gcc)
        kb = k * beta
        zf = z_ref[:, hs].astype(F32)
        heads.append(dict(q=q, k=k, kt=k.T, kb=kb, gcc=gcc, gcr=gcr, qe=q * egc,
                          rhs=jnp.concatenate([vs[:, hs] * beta, kb * egc], axis=1),
                          zg=zf * _sigmoid(zf), hs=hs))

    probs = [(i, c) for c in range(nc) for i in range(hb)]
    rsl = [slice(c * CHUNK, (c + 1) * CHUNK) for c in range(nc)]
    decay = [jnp.exp(jnp.where(tri, heads[i]["gcc"][rsl[c]] - heads[i]["gcr"][:, rsl[c]], NEG)) for i, c in probs]
    kq = [_mm_nt(jnp.concatenate([heads[i]["kb"][rsl[c]], heads[i]["q"][rsl[c]]], axis=0), heads[i]["k"][rsl[c]])
          for i, c in probs]
    a_mats = [jnp.where(strict, x[:CHUNK] * dc, 0.0) for x, dc in zip(kq, decay)]
    att = [x[CHUNK:] * dc for x, dc in zip(kq, decay)]
    t_minus_i = _unit_lower_inverse_minus_eye(a_mats, eye, blk16, blk32)
    rhs = [heads[i]["rhs"][rsl[c]] for i, c in probs]
    sol = [r + _mm(tm, r) for tm, r in zip(t_minus_i, rhs)]

    state = [s_ref[i] for i in range(hb)]
    for c in range(nc):
        rs = rsl[c]
        idx = [c * hb + i for i in range(hb)]
        ws = [_mm(jnp.concatenate([sol[p][:, B_DV:], heads[i]["qe"][rs]], axis=0), state[i])
              for i, p in enumerate(idx)]
        v_new = [sol[p][:, :B_DV] - w[:CHUNK] for p, w in zip(idx, ws)]
        g_last = [heads[i]["gcc"][rs][CHUNK - 1:CHUNK, :] for i in range(hb)]
        lhs = [jnp.concatenate([att[p], heads[i]["kt"][:, rs] * jnp.exp(g_last[i] - heads[i]["gcr"][:, rs])], axis=0)
               for i, p in enumerate(idx)]
        r2 = [_mm(l, v) for l, v in zip(lhs, v_new)]
        for i in range(hb):
            state[i] = state[i] * jnp.exp(g_last[i]) + r2[i][CHUNK:]
            o = ws[i][CHUNK:] + r2[i][:CHUNK]
            o = (o * lax.rsqrt(jnp.mean(o * o, axis=-1, keepdims=True) + EPS)) * gw
            o_ref[rs, heads[i]["hs"]] = (o * heads[i]["zg"][rs]).astype(o_ref.dtype)
    for i in range(hb):
        s_ref[i] = state[i]


def _gated_deltanet(proj, ba, conv_w, a_log, dt_bias, g_out, batch, seq, tt=256, hb=4):
    t = proj.shape[0]
    tt = min(tt, seq)
    width = hb * B_DK
    n_t = seq // tt

    def tok(off):
        j0 = off // width
        return pl.BlockSpec((tt, width), lambda b, g, j: (b * n_t + j, j0 + g))

    def cw(off):
        j0 = off // width
        return pl.BlockSpec((CONV_W, width), lambda b, g, j: (0, j0 + g))

    vec = pl.BlockSpec((1, LANES), lambda b, g, j: (0, 0))
    pad16 = jnp.zeros((1, LANES), F32)
    alog = pad16.at[0, B_HEADS:2 * B_HEADS].set(a_log)
    dtb = pad16.at[0, B_HEADS:2 * B_HEADS].set(dt_bias)
    return pl.pallas_call(
        functools.partial(_gdn_kernel, tt=tt, hb=hb),
        grid=(batch, B_HEADS // hb, n_t),
        in_specs=[tok(OFF_BQ), tok(OFF_BK), tok(OFF_BV), tok(OFF_BZ),
                  pl.BlockSpec((tt, LANES), lambda b, g, j: (b * n_t + j, 0)),
                  cw(0), cw(B_QK), cw(2 * B_QK), vec, vec, vec],
        out_specs=pl.BlockSpec((tt, width), lambda b, g, j: (b * n_t + j, g)),
        out_shape=jax.ShapeDtypeStruct((t, B_V), BF16),
        scratch_shapes=[pltpu.VMEM((tt + 2 * SUBLANES, width), F32),
                        pltpu.VMEM((tt + 2 * SUBLANES, width), F32),
                        pltpu.VMEM((tt + 2 * SUBLANES, width), F32),
                        pltpu.VMEM((hb, B_DK, B_DV), F32),
                        pltpu.VMEM((LANES, tt), F32)],
        compiler_params=_params(("parallel", "parallel", "arbitrary")),
        name="gated_deltanet",
    )(proj, proj, proj, proj, ba, conv_w, conv_w, conv_w, alog, dtb, g_out.reshape(1, B_DV))


def _split_w_in(w):
    wt = jnp.swapaxes(w, 0, 1).astype(BF16)
    return wt, wt[OFF_MQ + 2 * B_HEADS:]


def kernel(x, mem, rel_bias, g_mix, w_in, conv_w, a_log, dt_bias, g_dn_out, sinks, g_mem, w_mem_kv,
           w_br_a, w_br_b, w_br_m, w_o, g_ffn, w_ffn_in, w_ffn_out, g_final):
    batch, seq, d = x.shape
    mem_len = mem.shape[1]
    depth = w_in.shape[0]
    d_ff = w_ffn_out.shape[1]

    xt = x.reshape(batch * seq, d)
    memt = mem.reshape(batch * mem_len, d)
    bias = _bias_table(rel_bias)
    for l in range(depth):
        wt_tok, wt_gate = _split_w_in(w_in[l])
        h, ba = _rmsnorm_ba(xt, g_mix[l], wt_tok, OFF_MQ)
        proj = _matmul_nt(h, wt_tok, BF16, PROJ_TOK_BN, n=OFF_MQ)
        gates = _matmul_nt(h, wt_gate, BF16, PROJ_GATE_BN)

        o_a = _swa_attention(proj, sinks[l], bias, seq)
        o_b = _gated_deltanet(proj, ba, conv_w[l], a_log[l], dt_bias[l], g_dn_out[l], batch, seq)
        mem_kv = _matmul(_rmsnorm(memt, g_mem[l], BF16), w_mem_kv[l].astype(BF16), BF16, bn=512)
        o_m = _cross_attention(gates, mem_kv, seq, mem_len)

        y = _merge(o_a, o_b, o_m, w_br_a[l].astype(BF16), w_br_b[l].astype(BF16), w_br_m[l].astype(BF16),
                   gates, M_Q, M_Q + d, M_Q + 2 * d)
        xt = _matmul_residual(y, w_o[l].astype(BF16), xt)

        act = _swiglu_in(_rmsnorm(xt, g_ffn[l], BF16), w_ffn_in[l].astype(BF16), d_ff)
        xt = _matmul_residual(act, w_ffn_out[l].astype(BF16), xt, bm=512, bn=512)
    out = _rmsnorm(xt, g_final, F32)
    return out.reshape(batch, seq, d)
```

```python
import functools
import math

import numpy as np
import jax
import jax.numpy as jnp
from jax import lax
from jax.experimental import pallas as pl
from jax.experimental.pallas import tpu as pltpu

F32 = jnp.float32
BF16 = jnp.bfloat16

EPS = 1e-6
NEG = -1e30

A_HEADS = 32
A_KV_HEADS = 4
A_HEAD_DIM = 64
A_GROUP = A_HEADS // A_KV_HEADS
WINDOW = 128
BLOCK = 128
N_BUCKETS = 32
MAX_DISTANCE = 128
B_HEADS = 16
B_DK = 128
B_DV = 128
CONV_W = 4
CHUNK = 64
M_HEADS = 4
M_HEAD_DIM = 128

A_Q = A_HEADS * A_HEAD_DIM
A_KV = A_KV_HEADS * A_HEAD_DIM
B_QK = B_HEADS * B_DK
B_V = B_HEADS * B_DV
M_Q = M_HEADS * M_HEAD_DIM

LANES = 128
SUBLANES = 8
VMEM_LIMIT = 60 * 1024 * 1024

OFF_AQ = 0
OFF_AK = OFF_AQ + A_Q
OFF_AV = OFF_AK + A_KV
OFF_BQ = OFF_AV + A_KV
OFF_BK = OFF_BQ + B_QK
OFF_BV = OFF_BK + B_QK
OFF_BZ = OFF_BV + B_V
OFF_MQ = OFF_BZ + B_V

PROJ_TOK_BN = 1536
PROJ_GATE_BN = 1280


def _params(sem):
    return pltpu.CompilerParams(dimension_semantics=sem, vmem_limit_bytes=VMEM_LIMIT)


def _sigmoid(x):
    return 0.5 * jnp.tanh(0.5 * x) + 0.5


def _mm(a, b):
    return jnp.dot(a.astype(BF16), b.astype(BF16), preferred_element_type=F32)


def _mm_nt(a, b):
    return lax.dot_general(a.astype(BF16), b.astype(BF16), (((1,), (1,)), ((), ())),
                           preferred_element_type=F32)


def _mm_tn(a, b):
    return lax.dot_general(a.astype(BF16), b.astype(BF16), (((0,), (0,)), ((), ())),
                           preferred_element_type=F32)


def _rmsnorm_kernel(x_ref, g_ref, o_ref):
    x = x_ref[...]
    r = lax.rsqrt(jnp.mean(x * x, axis=-1, keepdims=True) + EPS)
    o_ref[...] = ((x * r) * g_ref[...]).astype(o_ref.dtype)


def _rmsnorm(x, g, out_dtype, rows=256):
    t, d = x.shape
    rows = min(rows, t)
    return pl.pallas_call(
        _rmsnorm_kernel,
        grid=(t // rows,),
        in_specs=[pl.BlockSpec((rows, d), lambda i: (i, 0)),
                  pl.BlockSpec((1, d), lambda i: (0, 0))],
        out_specs=pl.BlockSpec((rows, d), lambda i: (i, 0)),
        out_shape=jax.ShapeDtypeStruct((t, d), out_dtype),
        compiler_params=_params(("parallel",)),
        name="rmsnorm",
    )(x, g.reshape(1, d))


def _rmsnorm_ba_kernel(x_ref, g_ref, wba_ref, h_ref, ba_ref):
    x = x_ref[...]
    r = lax.rsqrt(jnp.mean(x * x, axis=-1, keepdims=True) + EPS)
    h = ((x * r) * g_ref[...]).astype(BF16)
    h_ref[...] = h
    ba_ref[...] = _mm_nt(h, wba_ref[...])


def _rmsnorm_ba(x, g, wt, ba_row, rows=256):
    t, d = x.shape
    assert ba_row % LANES == 0
    return pl.pallas_call(
        _rmsnorm_ba_kernel,
        grid=(t // rows,),
        in_specs=[pl.BlockSpec((rows, d), lambda i: (i, 0)),
                  pl.BlockSpec((1, d), lambda i: (0, 0)),
                  pl.BlockSpec((LANES, d), lambda i: (ba_row // LANES, 0))],
        out_specs=[pl.BlockSpec((rows, d), lambda i: (i, 0)),
                   pl.BlockSpec((rows, LANES), lambda i: (i, 0))],
        out_shape=[jax.ShapeDtypeStruct((t, d), BF16),
                   jax.ShapeDtypeStruct((t, LANES), F32)],
        compiler_params=_params(("parallel",)),
        name="rmsnorm_ba",
    )(x, g.reshape(1, d), wt)


def _mm_kernel(x_ref, w_ref, o_ref):
    o_ref[...] = jnp.dot(x_ref[...], w_ref[...], preferred_element_type=F32).astype(o_ref.dtype)


def _matmul(x, w, out_dtype, bm=1024, bn=1024):
    m, k = x.shape
    n = w.shape[1]
    bm = min(bm, m)
    return pl.pallas_call(
        _mm_kernel,
        grid=(m // bm, n // bn),
        in_specs=[pl.BlockSpec((bm, k), lambda i, j: (i, 0)),
                  pl.BlockSpec((k, bn), lambda i, j: (0, j))],
        out_specs=pl.BlockSpec((bm, bn), lambda i, j: (i, j)),
        out_shape=jax.ShapeDtypeStruct((m, n), out_dtype),
        compiler_params=_params(("parallel", "arbitrary")),
        name="matmul",
    )(x, w)


def _mm_nt_kernel(x_ref, wt_ref, o_ref):
    o_ref[...] = _mm_nt(x_ref[...], wt_ref[...]).astype(o_ref.dtype)


def _matmul_nt(x, wt, out_dtype, bn, n=None, bm=1024):
    m, k = x.shape
    n = wt.shape[0] if n is None else n
    bm = min(bm, m)
    assert n % bn == 0
    return pl.pallas_call(
        _mm_nt_kernel,
        grid=(m // bm, n // bn),
        in_specs=[pl.BlockSpec((bm, k), lambda i, j: (i, 0)),
                  pl.BlockSpec((bn, k), lambda i, j: (j, 0))],
        out_specs=pl.BlockSpec((bm, bn), lambda i, j: (i, j)),
        out_shape=jax.ShapeDtypeStruct((m, n), out_dtype),
        compiler_params=_params(("parallel", "arbitrary")),
        name="matmul_nt",
    )(x, wt)


def _mm_res_kernel(x_ref, w_ref, res_ref, o_ref):
    o_ref[...] = res_ref[...] + jnp.dot(x_ref[...], w_ref[...], preferred_element_type=F32)


def _matmul_residual(x, w, res, bm=1024, bn=1024):
    m, k = x.shape
    n = w.shape[1]
    bm = min(bm, m)
    return pl.pallas_call(
        _mm_res_kernel,
        grid=(m // bm, n // bn),
        in_specs=[pl.BlockSpec((bm, k), lambda i, j: (i, 0)),
                  pl.BlockSpec((k, bn), lambda i, j: (0, j)),
                  pl.BlockSpec((bm, bn), lambda i, j: (i, j))],
        out_specs=pl.BlockSpec((bm, bn), lambda i, j: (i, j)),
        out_shape=jax.ShapeDtypeStruct((m, n), F32),
        compiler_params=_params(("parallel", "arbitrary")),
        name="matmul_residual",
    )(x, w, res)


def _mm_res_prenorm_kernel(x_ref, w_ref, res_ref, g_ref, o_ref, xg_ref, r_ref, ssq_ref, *, nj):
    j = pl.program_id(1)
    y = res_ref[...] + jnp.dot(x_ref[...], w_ref[...], preferred_element_type=F32)
    o_ref[...] = y
    xg_ref[...] = (y * g_ref[...]).astype(xg_ref.dtype)

    @pl.when(j == 0)
    def _():
        ssq_ref[...] = jnp.zeros_like(ssq_ref)

    ssq_ref[...] += jnp.sum(y * y, axis=-1, keepdims=True)

    @pl.when(j == nj - 1)
    def _():
        r_ref[...] = lax.rsqrt(ssq_ref[...] * (1.0 / (nj * y.shape[1])) + EPS)


def _matmul_residual_prenorm(x, w, res, g, bm=1024, bn=512):
    m, k = x.shape
    n = w.shape[1]
    bm = min(bm, m)
    nj = n // bn
    return pl.pallas_call(
        functools.partial(_mm_res_prenorm_kernel, nj=nj),
        grid=(m // bm, nj),
        in_specs=[pl.BlockSpec((bm, k), lambda i, j: (i, 0)),
                  pl.BlockSpec((k, bn), lambda i, j: (0, j)),
                  pl.BlockSpec((bm, bn), lambda i, j: (i, j)),
                  pl.BlockSpec((1, bn), lambda i, j: (0, j))],
        out_specs=[pl.BlockSpec((bm, bn), lambda i, j: (i, j)),
                   pl.BlockSpec((bm, bn), lambda i, j: (i, j)),
                   pl.BlockSpec((bm, 1), lambda i, j: (i, 0))],
        out_shape=[jax.ShapeDtypeStruct((m, n), F32),
                   jax.ShapeDtypeStruct((m, n), BF16),
                   jax.ShapeDtypeStruct((m, 1), F32)],
        scratch_shapes=[pltpu.VMEM((bm, 1), F32)],
        compiler_params=_params(("parallel", "arbitrary")),
        name="matmul_residual_prenorm",
    )(x, w, res, g.reshape(1, n))


def _mm_res_norm_kernel(x_ref, w_ref, res_ref, g_ref, o_ref, *, nj, bn):
    j = pl.program_id(1)
    y = res_ref[...] + jnp.dot(x_ref[...], w_ref[...], preferred_element_type=F32)
    for jj in range(nj):
        @pl.when(j == jj)
        def _(jj=jj):
            o_ref[:, jj * bn:(jj + 1) * bn] = y

    @pl.when(j == nj - 1)
    def _():
        z = o_ref[...]
        r = lax.rsqrt(jnp.mean(z * z, axis=-1, keepdims=True) + EPS)
        o_ref[...] = (z * r) * g_ref[...]


def _matmul_residual_rmsnorm(x, w, res, g, bm=512, bn=256):
    m, k = x.shape
    n = w.shape[1]
    bm = min(bm, m)
    nj = n // bn
    return pl.pallas_call(
        functools.partial(_mm_res_norm_kernel, nj=nj, bn=bn),
        grid=(m // bm, nj),
        in_specs=[pl.BlockSpec((bm, k), lambda i, j: (i, 0)),
                  pl.BlockSpec((k, bn), lambda i, j: (0, j)),
                  pl.BlockSpec((bm, bn), lambda i, j: (i, j)),
                  pl.BlockSpec((1, n), lambda i, j: (0, 0))],
        out_specs=pl.BlockSpec((bm, n), lambda i, j: (i, 0)),
        out_shape=jax.ShapeDtypeStruct((m, n), F32),
        compiler_params=_params(("parallel", "arbitrary")),
        name="matmul_residual_rmsnorm",
    )(x, w, res, g.reshape(1, n))


def _swiglu_in_kernel(x_ref, r_ref, wg_ref, wu_ref, o_ref):
    x = x_ref[...]
    r = r_ref[...]
    g = r * jnp.dot(x, wg_ref[...], preferred_element_type=F32)
    u = r * jnp.dot(x, wu_ref[...], preferred_element_type=F32)
    o_ref[...] = ((g * _sigmoid(g)) * u).astype(o_ref.dtype)


def _swiglu_in(x, r, w, n_ff, bm=2048, bn=256):
    m, k = x.shape
    bm = min(bm, m)
    nb = n_ff // bn
    assert nb * bn == n_ff
    return pl.pallas_call(
        _swiglu_in_kernel,
        grid=(m // bm, nb),
        in_specs=[pl.BlockSpec((bm, k), lambda i, j: (i, 0)),
                  pl.BlockSpec((bm, 1), lambda i, j: (i, 0)),
                  pl.BlockSpec((k, bn), lambda i, j: (0, j)),
                  pl.BlockSpec((k, bn), lambda i, j: (0, j + nb))],
        out_specs=pl.BlockSpec((bm, bn), lambda i, j: (i, j)),
        out_shape=jax.ShapeDtypeStruct((m, n_ff), BF16),
        compiler_params=_params(("parallel", "arbitrary")),
        name="swiglu_in",
    )(x, r, w, w)


def _merge_kernel(oa_ref, ob_ref, om_ref, wa_ref, wb_ref, wm_ref, ga_ref, gb_ref, gm_ref, y_ref):
    ya = jnp.dot(oa_ref[...], wa_ref[...], preferred_element_type=F32)
    y = _sigmoid(ga_ref[...].astype(F32)) * ya
    yb = jnp.dot(ob_ref[...], wb_ref[...], preferred_element_type=F32)
    y = y + _sigmoid(gb_ref[...].astype(F32)) * yb
    ym = jnp.dot(om_ref[...], wm_ref[...], preferred_element_type=F32)
    y = y + _sigmoid(gm_ref[...].astype(F32)) * ym
    y_ref[...] = y.astype(y_ref.dtype)


def _merge(o_a, o_b, o_m, w_a, w_b, w_m, proj, off_ga, off_gb, off_gm, bm=1024, bn=512):
    m = o_a.shape[0]
    n = w_a.shape[1]
    bm = min(bm, m)
    ja, jb, jm = off_ga // bn, off_gb // bn, off_gm // bn

    def row(width):
        return pl.BlockSpec((bm, width), lambda i, j: (i, 0))

    def col(height):
        return pl.BlockSpec((height, bn), lambda i, j: (0, j))

    def gate(j0):
        return pl.BlockSpec((bm, bn), lambda i, j: (i, j0 + j))

    return pl.pallas_call(
        _merge_kernel,
        grid=(m // bm, n // bn),
        in_specs=[row(o_a.shape[1]), row(o_b.shape[1]), row(o_m.shape[1]),
                  col(w_a.shape[0]), col(w_b.shape[0]), col(w_m.shape[0]),
                  gate(ja), gate(jb), gate(jm)],
        out_specs=pl.BlockSpec((bm, bn), lambda i, j: (i, j)),
        out_shape=jax.ShapeDtypeStruct((m, n), BF16),
        compiler_params=_params(("parallel", "arbitrary")),
        name="merge",
    )(o_a, o_b, o_m, w_a, w_b, w_m, proj, proj, proj)


def _t5_bucket_table():
    qi = np.arange(BLOCK)[:, None]
    kj = np.arange(2 * BLOCK)[None, :]
    dist = qi + BLOCK - kj
    band = (dist >= 0) & (dist < WINDOW)
    n = np.maximum(dist, 0)
    max_exact = N_BUCKETS // 2
    nf = np.maximum(n, 1).astype(np.float32)
    large = max_exact + (np.log(nf / np.float32(max_exact)) / np.float32(math.log(MAX_DISTANCE / max_exact))
                         * np.float32(N_BUCKETS - max_exact)).astype(np.int32)
    large = np.minimum(large, N_BUCKETS - 1)
    bucket = np.where(n < max_exact, n, large)
    return np.where(band, bucket, -1).astype(np.int32)


def _bias_table_kernel(rel_ref, idx_ref, o_ref):
    first = pl.program_id(0)
    h = pl.program_id(1)
    idx = idx_ref[...]
    acc = jnp.full(idx.shape, NEG, F32)
    for b in range(N_BUCKETS):
        acc = jnp.where(idx == b, rel_ref[b, h], acc)
    col = lax.broadcasted_iota(jnp.int32, idx.shape, 1)
    o_ref[0, 0] = jnp.where(jnp.logical_and(first == 1, col < BLOCK), NEG, acc)


def _bias_table(rel_bias):
    idx = jnp.asarray(_t5_bucket_table())
    return pl.pallas_call(
        _bias_table_kernel,
        grid=(2, A_HEADS),
        in_specs=[pl.BlockSpec(memory_space=pltpu.SMEM),
                  pl.BlockSpec((BLOCK, 2 * BLOCK), lambda f, h: (0, 0))],
        out_specs=pl.BlockSpec((1, 1, BLOCK, 2 * BLOCK), lambda f, h: (f, h, 0, 0)),
        out_shape=jax.ShapeDtypeStruct((2, A_HEADS, BLOCK, 2 * BLOCK), F32),
        compiler_params=_params(("arbitrary", "arbitrary")),
        name="swa_bias_table",
    )(rel_bias, idx)


def _swa_kernel(sink_ref, q_ref, kc_ref, kp_ref, vc_ref, vp_ref, bias_ref, o_ref):
    assert 2 * A_HEAD_DIM == LANES and A_GROUP % 2 == 0
    n_pair = A_GROUP // 2
    low = lax.broadcasted_iota(jnp.int32, (1, LANES), 1) < A_HEAD_DIM
    scale = A_HEAD_DIM ** -0.5
    ones_low = jnp.where(low, 1.0, 0.0).astype(F32) + jnp.zeros((2 * BLOCK, LANES), F32)
    ones_high = 1.0 - ones_low

    def kv_halves(cur_ref, prev_ref, kh):
        sl = slice((kh // 2) * LANES, (kh // 2 + 1) * LANES)
        w = jnp.concatenate([prev_ref[:, sl], cur_ref[:, sl]], axis=0).astype(F32)
        swapped = pltpu.roll(w, A_HEAD_DIM, axis=1)
        in_low, in_high = (w, swapped) if kh % 2 == 0 else (swapped, w)
        return jnp.where(low, in_low, 0.0), jnp.where(low, 0.0, in_high)

    scores = []
    for kh in range(A_KV_HEADS):
        qg = jnp.concatenate([q_ref[:, (kh * n_pair + m) * LANES:(kh * n_pair + m + 1) * LANES]
                              for m in range(n_pair)], axis=0)
        k_even, k_odd = kv_halves(kc_ref, kp_ref, kh)
        scores.append((_mm_nt(qg, k_even * scale), _mm_nt(qg, k_odd * scale)))

    for kh in range(A_KV_HEADS):
        v_even, v_odd = kv_halves(vc_ref, vp_ref, kh)
        probs = ([], [])
        sink_terms = []
        for m in range(n_pair):
            rows = slice(m * BLOCK, (m + 1) * BLOCK)
            terms = []
            for parity in range(2):
                h = kh * A_GROUP + 2 * m + parity
                s = scores[kh][parity][rows] + bias_ref[0, h]
                mx = jnp.maximum(jnp.max(s, axis=-1, keepdims=True), sink_ref[h])
                probs[parity].append(jnp.exp(s - mx).astype(BF16))
                terms.append(jnp.exp(sink_ref[h] - mx))
            sink_terms.append(jnp.where(low, terms[0], terms[1]))
        res = (_mm(jnp.concatenate(probs[0], axis=0), jnp.concatenate([v_even, ones_low], axis=1))
               + _mm(jnp.concatenate(probs[1], axis=0), jnp.concatenate([v_odd, ones_high], axis=1)))
        for m in range(n_pair):
            rows = slice(m * BLOCK, (m + 1) * BLOCK)
            o = res[rows, :LANES] / (res[rows, LANES:] + sink_terms[m])
            o_ref[:, (kh * n_pair + m) * LANES:(kh * n_pair + m + 1) * LANES] = o.astype(o_ref.dtype)


def _swa_attention(proj, sinks, bias, seq):
    t = proj.shape[0]
    nb = seq // BLOCK
    jk, jv = OFF_AK // A_KV, OFF_AV // A_KV

    def cur(jc):
        return pl.BlockSpec((BLOCK, A_KV), lambda i: (i, jc))

    def prev(jc):
        return pl.BlockSpec((BLOCK, A_KV), lambda i: (jnp.maximum(i - 1, 0), jc))

    return pl.pallas_call(
        _swa_kernel,
        grid=(t // BLOCK,),
        in_specs=[pl.BlockSpec(memory_space=pltpu.SMEM),
                  pl.BlockSpec((BLOCK, A_Q), lambda i: (i, OFF_AQ // A_Q)),
                  cur(jk), prev(jk), cur(jv), prev(jv),
                  pl.BlockSpec((1, A_HEADS, BLOCK, 2 * BLOCK),
                               lambda i: (jnp.where(i % nb == 0, 1, 0), 0, 0, 0))],
        out_specs=pl.BlockSpec((BLOCK, A_Q), lambda i: (i, 0)),
        out_shape=jax.ShapeDtypeStruct((t, A_Q), BF16),
        compiler_params=_params(("parallel",)),
        name="swa_attention",
    )(sinks, proj, proj, proj, proj, proj, bias)


def _xattn_kernel(q_ref, kv_ref, o_ref):
    scale = M_HEAD_DIM ** -0.5
    for h in range(M_HEADS):
        hs = slice(h * M_HEAD_DIM, (h + 1) * M_HEAD_DIM)
        vs = slice(M_Q + h * M_HEAD_DIM, M_Q + (h + 1) * M_HEAD_DIM)
        s = _mm_nt(q_ref[:, hs], kv_ref[:, hs]) * scale
        m = jnp.max(s, axis=-1, keepdims=True)
        p = jnp.exp(s - m)
        denom = jnp.sum(p, axis=-1, keepdims=True)
        o = _mm(p, kv_ref[:, vs]) * (1.0 / denom)
        o_ref[:, hs] = o.astype(o_ref.dtype)


def _cross_attention(gates, mem_kv, seq, mem_len, rows=512):
    t = gates.shape[0]
    rows = min(rows, seq)
    per_batch = seq // rows
    return pl.pallas_call(
        _xattn_kernel,
        grid=(t // rows,),
        in_specs=[pl.BlockSpec((rows, M_Q), lambda i: (i, 0)),
                  pl.BlockSpec((mem_len, 2 * M_Q), lambda i: (i // per_batch, 0))],
        out_specs=pl.BlockSpec((rows, M_Q), lambda i: (i, 0)),
        out_shape=jax.ShapeDtypeStruct((t, M_Q), BF16),
        compiler_params=_params(("parallel",)),
        name="cross_attention",
    )(gates, mem_kv)


def _unit_lower_inverse_minus_eye(a_mats, eye, blk16, blk32):
    assert CHUNK == 64
    nd = [jnp.where(blk16, -a, 0.0) for a in a_mats]
    p = [eye + n for n in nd]
    pw = [_mm(n, n) for n in nd]
    for _ in range(2):
        r = [_mm(jnp.concatenate([w, q], axis=0), w) for w, q in zip(pw, p)]
        pw = [x[:CHUNK] for x in r]
        p = [q + x[CHUNK:] for q, x in zip(p, r)]
    d = [q + _mm(q, w) for q, w in zip(p, pw)]
    for inner, outer in ((blk16, blk32), (blk32, None)):
        if outer is None:
            c_off = [jnp.where(inner, 0.0, a) for a in a_mats]
        else:
            c_off = [jnp.where(jnp.logical_and(outer, jnp.logical_not(inner)), a, 0.0) for a in a_mats]
        x = [_mm(dd, c) for dd, c in zip(d, c_off)]
        d = [dd - _mm(xx, dd) for dd, xx in zip(d, x)]
    return [dd - eye for dd in d]


def _gdn_kernel(q_ref, k_ref, v_ref, z_ref, ba_ref, cwq_ref, cwk_ref, cwv_ref, alog_ref, dtb_ref, gout_ref,
                o_ref, xq_ref, xk_ref, xv_ref, s_ref, gt_ref, *, tt, hb):
    hg = pl.program_id(1)
    j = pl.program_id(2)
    pad = SUBLANES
    nc = tt // CHUNK

    @pl.when(j == 0)
    def _():
        for r in (xq_ref, xk_ref, xv_ref):
            r[0:pad, :] = jnp.zeros((pad, r.shape[1]), F32)
        s_ref[...] = jnp.zeros_like(s_ref)

    @pl.when(j > 0)
    def _():
        for r in (xq_ref, xk_ref, xv_ref):
            r[0:pad, :] = r[tt:tt + pad, :]

    def conv_silu(x_ref, xe_ref, cw_ref):
        xe_ref[pad:pad + tt, :] = x_ref[...].astype(F32)
        w = cw_ref[...]
        xe = xe_ref[0:pad + tt, :]
        taps = [pltpu.roll(xe, CONV_W - 1 - i, axis=0)[pad:] for i in range(CONV_W - 1)] + [xe[pad:]]
        y = taps[0] * w[0:1, :]
        for i in range(1, CONV_W):
            y = y + taps[i] * w[i:i + 1, :]
        return y * _sigmoid(y)

    qs = conv_silu(q_ref, xq_ref, cwq_ref)
    ks = conv_silu(k_ref, xk_ref, cwk_ref)
    vs = conv_silu(v_ref, xv_ref, cwv_ref)

    ba = ba_ref[...]
    lane = lax.broadcasted_iota(jnp.int32, (tt, LANES), 1)
    pos = lax.broadcasted_iota(jnp.int32, (tt, LANES), 0) % CHUNK
    beta_all = _sigmoid(ba)
    xg = ba + dtb_ref[...]
    softplus = jnp.maximum(xg, 0.0) + jnp.log1p(jnp.exp(-jnp.abs(xg)))
    gc_all = -jnp.exp(alog_ref[...]) * softplus
    step = 1
    while step < CHUNK:
        gc_all = gc_all + jnp.where(pos >= step, pltpu.roll(gc_all, step, axis=0), 0.0)
        step *= 2
    gt_ref[...] = gc_all.T

    row = lax.broadcasted_iota(jnp.int32, (CHUNK, CHUNK), 0)
    colm = lax.broadcasted_iota(jnp.int32, (CHUNK, CHUNK), 1)
    tri = row >= colm
    strict = row > colm
    blk16 = (row // 16) == (colm // 16)
    blk32 = (row // 32) == (colm // 32)
    eye = jnp.where(row == colm, 1.0, 0.0).astype(F32)

    def pick(arr, l):
        return jnp.sum(jnp.where(lane == l, arr, 0.0), axis=1, keepdims=True)

    gw = gout_ref[...]
    heads = []
    for i in range(hb):
        h = hg * hb + i
        hs = slice(i * B_DK, (i + 1) * B_DK)
        q = qs[:, hs]
        k = ks[:, hs]
        q = (q * lax.rsqrt(jnp.sum(q * q, axis=-1, keepdims=True) + EPS)) * (B_DK ** -0.5)
        k = k * lax.rsqrt(jnp.sum(k * k, axis=-1, keepdims=True) + EPS)
        beta = pick(beta_all, h)
        gcc = pick(gc_all, B_HEADS + h)
        gcr = gt_ref[pl.ds(B_HEADS + h, 1), :]
        egc = jnp.exp(gcc)
        kb = k * beta
        zf = z_ref[:, hs].astype(F32)
        heads.append(dict(q=q, k=k, kt=k.T, kb=kb, gcc=gcc, gcr=gcr, qe=q * egc,
                          rhs=jnp.concatenate([vs[:, hs] * beta, kb * egc], axis=1),
                          zg=zf * _sigmoid(zf), hs=hs))

    probs = [(i, c) for c in range(nc) for i in range(hb)]
    rsl = [slice(c * CHUNK, (c + 1) * CHUNK) for c in range(nc)]
    decay = [jnp.exp(jnp.where(tri, heads[i]["gcc"][rsl[c]] - heads[i]["gcr"][:, rsl[c]], NEG)) for i, c in probs]
    kq = [_mm_nt(jnp.concatenate([heads[i]["kb"][rsl[c]], heads[i]["q"][rsl[c]]], axis=0), heads[i]["k"][rsl[c]])
          for i, c in probs]
    a_mats = [jnp.where(strict, x[:CHUNK] * dc, 0.0) for x, dc in zip(kq, decay)]
    att = [x[CHUNK:] * dc for x, dc in zip(kq, decay)]
    t_minus_i = _unit_lower_inverse_minus_eye(a_mats, eye, blk16, blk32)
    rhs = [heads[i]["rhs"][rsl[c]] for i, c in probs]
    sol = [r + _mm(tm, r) for tm, r in zip(t_minus_i, rhs)]

    state = [s_ref[i] for i in range(hb)]
    for c in range(nc):
        rs = rsl[c]
        idx = [c * hb + i for i in range(hb)]
        ws = [_mm(jnp.concatenate([sol[p][:, B_DV:], heads[i]["qe"][rs]], axis=0), state[i])
              for i, p in enumerate(idx)]
        v_new = [sol[p][:, :B_DV] - w[:CHUNK] for p, w in zip(idx, ws)]
        g_last = [heads[i]["gcc"][rs][CHUNK - 1:CHUNK, :] for i in range(hb)]
        lhs = [jnp.concatenate([att[p], heads[i]["kt"][:, rs] * jnp.exp(g_last[i] - heads[i]["gcr"][:, rs])], axis=0)
               for i, p in enumerate(idx)]
        r2 = [_mm(l, v) for l, v in zip(lhs, v_new)]
        for i in range(hb):
            state[i] = state[i] * jnp.exp(g_last[i]) + r2[i][CHUNK:]
            o = ws[i][CHUNK:] + r2[i][:CHUNK]
            o = (o * lax.rsqrt(jnp.mean(o * o, axis=-1, keepdims=True) + EPS)) * gw
            o_ref[rs, heads[i]["hs"]] = (o * heads[i]["zg"][rs]).astype(o_ref.dtype)
    for i in range(hb):
        s_ref[i] = state[i]


def _gated_deltanet(proj, ba, conv_w, a_log, dt_bias, g_out, batch, seq, tt=256, hb=4):
    t = proj.shape[0]
    tt = min(tt, seq)
    width = hb * B_DK
    n_t = seq // tt

    def tok(off):
        j0 = off // width
        return pl.BlockSpec((tt, width), lambda b, g, j: (b * n_t + j, j0 + g))

    def cw(off):
        j0 = off // width
        return pl.BlockSpec((CONV_W, width), lambda b, g, j: (0, j0 + g))

    vec = pl.BlockSpec((1, LANES), lambda b, g, j: (0, 0))
    pad16 = jnp.zeros((1, LANES), F32)
    alog = pad16.at[0, B_HEADS:2 * B_HEADS].set(a_log)
    dtb = pad16.at[0, B_HEADS:2 * B_HEADS].set(dt_bias)
    return pl.pallas_call(
        functools.partial(_gdn_kernel, tt=tt, hb=hb),
        grid=(batch, B_HEADS // hb, n_t),
        in_specs=[tok(OFF_BQ), tok(OFF_BK), tok(OFF_BV), tok(OFF_BZ),
                  pl.BlockSpec((tt, LANES), lambda b, g, j: (b * n_t + j, 0)),
                  cw(0), cw(B_QK), cw(2 * B_QK), vec, vec, vec],
        out_specs=pl.BlockSpec((tt, width), lambda b, g, j: (b * n_t + j, g)),
        out_shape=jax.ShapeDtypeStruct((t, B_V), BF16),
        scratch_shapes=[pltpu.VMEM((tt + 2 * SUBLANES, width), F32),
                        pltpu.VMEM((tt + 2 * SUBLANES, width), F32),
                        pltpu.VMEM((tt + 2 * SUBLANES, width), F32),
                        pltpu.VMEM((hb, B_DK, B_DV), F32),
                        pltpu.VMEM((LANES, tt), F32)],
        compiler_params=_params(("parallel", "parallel", "arbitrary")),
        name="gated_deltanet",
    )(proj, proj, proj, proj, ba, conv_w, conv_w, conv_w, alog, dtb, g_out.reshape(1, B_DV))


def _split_w_in(w):
    wt = jnp.swapaxes(w, 0, 1).astype(BF16)
    return wt, wt[OFF_MQ + 2 * B_HEADS:]


def kernel(x, mem, rel_bias, g_mix, w_in, conv_w, a_log, dt_bias, g_dn_out, sinks, g_mem, w_mem_kv,
           w_br_a, w_br_b, w_br_m, w_o, g_ffn, w_ffn_in, w_ffn_out, g_final):
    batch, seq, d = x.shape
    mem_len = mem.shape[1]
    depth = w_in.shape[0]
    d_ff = w_ffn_out.shape[1]

    xt = x.reshape(batch * seq, d)
    memt = mem.reshape(batch * mem_len, d)
    bias = _bias_table(rel_bias)
    for l in range(depth):
        wt_tok, wt_gate = _split_w_in(w_in[l])
        h, ba = _rmsnorm_ba(xt, g_mix[l], wt_tok, OFF_MQ)
        proj = _matmul_nt(h, wt_tok, BF16, PROJ_TOK_BN, n=OFF_MQ)
        gates = _matmul_nt(h, wt_gate, BF16, PROJ_GATE_BN)

        o_a = _swa_attention(proj, sinks[l], bias, seq)
        o_b = _gated_deltanet(proj, ba, conv_w[l], a_log[l], dt_bias[l], g_dn_out[l], batch, seq)
        mem_kv = _matmul(_rmsnorm(memt, g_mem[l], BF16), w_mem_kv[l].astype(BF16), BF16, bn=512)
        o_m = _cross_attention(gates, mem_kv, seq, mem_len)

        y = _merge(o_a, o_b, o_m, w_br_a[l].astype(BF16), w_br_b[l].astype(BF16), w_br_m[l].astype(BF16),
                   gates, M_Q, M_Q + d, M_Q + 2 * d)
        xt, xg, r = _matmul_residual_prenorm(y, w_o[l].astype(BF16), xt, g_ffn[l])

        act = _swiglu_in(xg, r, w_ffn_in[l].astype(BF16), d_ff)
        if l + 1 < depth:
            xt = _matmul_residual(act, w_ffn_out[l].astype(BF16), xt, bm=512, bn=512)
        else:
            out = _matmul_residual_rmsnorm(act, w_ffn_out[l].astype(BF16), xt, g_final)
    return out.reshape(batch, seq, d)
```

```python
import functools
import math

import numpy as np
import jax
import jax.numpy as jnp
from jax import lax
from jax.experimental import pallas as pl
from jax.experimental.pallas import tpu as pltpu

F32 = jnp.float32
BF16 = jnp.bfloat16

EPS = 1e-6
NEG = -1e30

A_HEADS = 32
A_KV_HEADS = 4
A_HEAD_DIM = 64
A_GROUP = A_HEADS // A_KV_HEADS
WINDOW = 128
BLOCK = 128
N_BUCKETS = 32
MAX_DISTANCE = 128
B_HEADS = 16
B_DK = 128
B_DV = 128
CONV_W = 4
CHUNK = 64
M_HEADS = 4
M_HEAD_DIM = 128

A_Q = A_HEADS * A_HEAD_DIM
A_KV = A_KV_HEADS * A_HEAD_DIM
B_QK = B_HEADS * B_DK
B_V = B_HEADS * B_DV
M_Q = M_HEADS * M_HEAD_DIM

LANES = 128
SUBLANES = 8
VMEM_LIMIT = 60 * 1024 * 1024
EPILOGUE_CHUNKS = 4

OFF_AQ = 0
OFF_AK = OFF_AQ + A_Q
OFF_AV = OFF_AK + A_KV
OFF_BQ = OFF_AV + A_KV
OFF_BK = OFF_BQ + B_QK
OFF_BV = OFF_BK + B_QK
OFF_BZ = OFF_BV + B_V
OFF_MQ = OFF_BZ + B_V

PROJ_TOK_BN = 768
PROJ_GATE_BN = 1280


def _params(sem):
    return pltpu.CompilerParams(dimension_semantics=sem, vmem_limit_bytes=VMEM_LIMIT)


def _sigmoid(x):
    return 0.5 * jnp.tanh(0.5 * x) + 0.5


def _mm(a, b):
    return jnp.dot(a.astype(BF16), b.astype(BF16), preferred_element_type=F32)


def _mm_nt(a, b):
    return lax.dot_general(a.astype(BF16), b.astype(BF16), (((1,), (1,)), ((), ())),
                           preferred_element_type=F32)


def _mm_tn(a, b):
    return lax.dot_general(a.astype(BF16), b.astype(BF16), (((0,), (0,)), ((), ())),
                           preferred_element_type=F32)


def _rmsnorm_kernel(x_ref, g_ref, o_ref):
    x = x_ref[...]
    r = lax.rsqrt(jnp.mean(x * x, axis=-1, keepdims=True) + EPS)
    o_ref[...] = ((x * r) * g_ref[...]).astype(o_ref.dtype)


def _rmsnorm(x, g, out_dtype, rows=256):
    t, d = x.shape
    rows = min(rows, t)
    return pl.pallas_call(
        _rmsnorm_kernel,
        grid=(t // rows,),
        in_specs=[pl.BlockSpec((rows, d), lambda i: (i, 0)),
                  pl.BlockSpec((1, d), lambda i: (0, 0))],
        out_specs=pl.BlockSpec((rows, d), lambda i: (i, 0)),
        out_shape=jax.ShapeDtypeStruct((t, d), out_dtype),
        compiler_params=_params(("parallel",)),
        name="rmsnorm",
    )(x, g.reshape(1, d))


def _rmsnorm_ba_kernel(x_ref, g_ref, wba_ref, h_ref, ba_ref):
    x = x_ref[...]
    r = lax.rsqrt(jnp.mean(x * x, axis=-1, keepdims=True) + EPS)
    h = ((x * r) * g_ref[...]).astype(BF16)
    h_ref[...] = h
    ba_ref[...] = _mm_nt(h, wba_ref[...])


def _rmsnorm_ba(x, g, wt, ba_row, rows=256):
    t, d = x.shape
    assert ba_row % LANES == 0
    return pl.pallas_call(
        _rmsnorm_ba_kernel,
        grid=(t // rows,),
        in_specs=[pl.BlockSpec((rows, d), lambda i: (i, 0)),
                  pl.BlockSpec((1, d), lambda i: (0, 0)),
                  pl.BlockSpec((LANES, d), lambda i: (ba_row // LANES, 0))],
        out_specs=[pl.BlockSpec((rows, d), lambda i: (i, 0)),
                   pl.BlockSpec((rows, LANES), lambda i: (i, 0))],
        out_shape=[jax.ShapeDtypeStruct((t, d), BF16),
                   jax.ShapeDtypeStruct((t, LANES), F32)],
        compiler_params=_params(("parallel",)),
        name="rmsnorm_ba",
    )(x, g.reshape(1, d), wt)


def _mm_kernel(x_ref, w_ref, o_ref):
    o_ref[...] = jnp.dot(x_ref[...], w_ref[...], preferred_element_type=F32).astype(o_ref.dtype)


def _matmul(x, w, out_dtype, bm=1024, bn=1024):
    m, k = x.shape
    n = w.shape[1]
    bm = min(bm, m)
    return pl.pallas_call(
        _mm_kernel,
        grid=(m // bm, n // bn),
        in_specs=[pl.BlockSpec((bm, k), lambda i, j: (i, 0)),
                  pl.BlockSpec((k, bn), lambda i, j: (0, j))],
        out_specs=pl.BlockSpec((bm, bn), lambda i, j: (i, j)),
        out_shape=jax.ShapeDtypeStruct((m, n), out_dtype),
        compiler_params=_params(("parallel", "arbitrary")),
        name="matmul",
    )(x, w)


def _mm_nt_kernel(*refs, n_side, side_blocks, nj):
    x_ref, wt_ref = refs[:2]
    side_in = refs[2:2 + n_side]
    o_ref = refs[2 + n_side]
    side_out = refs[3 + n_side:]
    o_ref[...] = _mm_nt(x_ref[...], wt_ref[...]).astype(o_ref.dtype)
    step = pl.program_id(0) * nj + pl.program_id(1)
    for s in range(n_side):
        @pl.when(step < side_blocks[s])
        def _(s=s):
            side_out[s][...] = side_in[s][...].astype(BF16)


def _side_rows(total, rows, steps):
    while total // rows > steps:
        rows *= 2
    assert total % rows == 0
    return rows


def _matmul_nt(x, wt, out_dtype, bn, n, bm=1024, sides=()):
    m, k = x.shape
    bm = min(bm, m)
    assert n % bn == 0
    nj = n // bn
    steps = (m // bm) * nj
    sides = [(a, _side_rows(a.shape[0], rows, steps)) for a, rows in sides]
    side_blocks = tuple(a.shape[0] // rows for a, rows in sides)

    def side_spec(a, rows, nb):
        return pl.BlockSpec((rows, a.shape[1]), lambda i, j: (jnp.minimum(i * nj + j, nb - 1), 0))

    side_specs = [side_spec(a, rows, nb) for (a, rows), nb in zip(sides, side_blocks)]
    outs = pl.pallas_call(
        functools.partial(_mm_nt_kernel, n_side=len(sides), side_blocks=side_blocks, nj=nj),
        grid=(m // bm, nj),
        in_specs=[pl.BlockSpec((bm, k), lambda i, j: (i, 0)),
                  pl.BlockSpec((bn, k), lambda i, j: (j, 0))] + side_specs,
        out_specs=[pl.BlockSpec((bm, bn), lambda i, j: (i, j))] + side_specs,
        out_shape=[jax.ShapeDtypeStruct((m, n), out_dtype)]
        + [jax.ShapeDtypeStruct(a.shape, BF16) for a, _ in sides],
        compiler_params=_params(("arbitrary", "arbitrary")),
        name="matmul_nt",
    )(x, wt, *[a for a, _ in sides])
    return (outs[0], list(outs[1:])) if sides else outs[0]


def _mm_res_kernel(x_ref, w_ref, res_ref, o_ref):
    o_ref[...] = res_ref[...] + jnp.dot(x_ref[...], w_ref[...], preferred_element_type=F32)


def _matmul_residual(x, w, res, bm=1024, bn=1024):
    m, k = x.shape
    n = w.shape[1]
    bm = min(bm, m)
    return pl.pallas_call(
        _mm_res_kernel,
        grid=(m // bm, n // bn),
        in_specs=[pl.BlockSpec((bm, k), lambda i, j: (i, 0)),
                  pl.BlockSpec((k, bn), lambda i, j: (0, j)),
                  pl.BlockSpec((bm, bn), lambda i, j: (i, j))],
        out_specs=pl.BlockSpec((bm, bn), lambda i, j: (i, j)),
        out_shape=jax.ShapeDtypeStruct((m, n), F32),
        compiler_params=_params(("parallel", "arbitrary")),
        name="matmul_residual",
    )(x, w, res)


def _mm_res_prenorm_kernel(x_ref, w_ref, res_ref, g_ref, o_ref, xg_ref, r_ref, ssq_ref, *, nj):
    j = pl.program_id(1)

    @pl.when(j == 0)
    def _():
        ssq_ref[...] = jnp.zeros_like(ssq_ref)

    rows = x_ref.shape[0] // EPILOGUE_CHUNKS
    for c in range(EPILOGUE_CHUNKS):
        rs = slice(c * rows, (c + 1) * rows)
        y = res_ref[rs, :] + jnp.dot(x_ref[rs, :], w_ref[...], preferred_element_type=F32)
        o_ref[rs, :] = y
        xg_ref[rs, :] = (y * g_ref[...]).astype(xg_ref.dtype)
        ssq_ref[rs, :] += jnp.sum(y * y, axis=-1, keepdims=True)

    @pl.when(j == nj - 1)
    def _():
        r_ref[...] = lax.rsqrt(ssq_ref[...] * (1.0 / (nj * o_ref.shape[1])) + EPS)


def _matmul_residual_prenorm(x, w, res, g, bm=1024, bn=512):
    m, k = x.shape
    n = w.shape[1]
    bm = min(bm, m)
    nj = n // bn
    return pl.pallas_call(
        functools.partial(_mm_res_prenorm_kernel, nj=nj),
        grid=(m // bm, nj),
        in_specs=[pl.BlockSpec((bm, k), lambda i, j: (i, 0)),
                  pl.BlockSpec((k, bn), lambda i, j: (0, j)),
                  pl.BlockSpec((bm, bn), lambda i, j: (i, j)),
                  pl.BlockSpec((1, bn), lambda i, j: (0, j))],
        out_specs=[pl.BlockSpec((bm, bn), lambda i, j: (i, j)),
                   pl.BlockSpec((bm, bn), lambda i, j: (i, j)),
                   pl.BlockSpec((bm, 1), lambda i, j: (i, 0))],
        out_shape=[jax.ShapeDtypeStruct((m, n), F32),
                   jax.ShapeDtypeStruct((m, n), BF16),
                   jax.ShapeDtypeStruct((m, 1), F32)],
        scratch_shapes=[pltpu.VMEM((bm, 1), F32)],
        compiler_params=_params(("parallel", "arbitrary")),
        name="matmul_residual_prenorm",
    )(x, w, res, g.reshape(1, n))


def _swiglu_in_kernel(x_ref, r_ref, wg_ref, wu_ref, o_ref):
    rows = x_ref.shape[0] // (2 * EPILOGUE_CHUNKS)
    for c in range(2 * EPILOGUE_CHUNKS):
        rs = slice(c * rows, (c + 1) * rows)
        x = x_ref[rs, :]
        r = r_ref[rs, :]
        g = r * jnp.dot(x, wg_ref[...], preferred_element_type=F32)
        u = r * jnp.dot(x, wu_ref[...], preferred_element_type=F32)
        o_ref[rs, :] = ((g * _sigmoid(g)) * u).astype(o_ref.dtype)


def _swiglu_in(x, r, w, n_ff, bm=2048, bn=256):
    m, k = x.shape
    bm = min(bm, m)
    nb = n_ff // bn
    assert nb * bn == n_ff
    return pl.pallas_call(
        _swiglu_in_kernel,
        grid=(m // bm, nb),
        in_specs=[pl.BlockSpec((bm, k), lambda i, j: (i, 0)),
                  pl.BlockSpec((bm, 1), lambda i, j: (i, 0)),
                  pl.BlockSpec((k, bn), lambda i, j: (0, j)),
                  pl.BlockSpec((k, bn), lambda i, j: (0, j + nb))],
        out_specs=pl.BlockSpec((bm, bn), lambda i, j: (i, j)),
        out_shape=jax.ShapeDtypeStruct((m, n_ff), BF16),
        compiler_params=_params(("parallel", "arbitrary")),
        name="swiglu_in",
    )(x, r, w, w)


def _merge_kernel(oa_ref, ob_ref, om_ref, wa_ref, wb_ref, wm_ref, ga_ref, gb_ref, gm_ref, y_ref):
    rows = oa_ref.shape[0] // EPILOGUE_CHUNKS
    for c in range(EPILOGUE_CHUNKS):
        rs = slice(c * rows, (c + 1) * rows)
        ya = jnp.dot(oa_ref[rs, :], wa_ref[...], preferred_element_type=F32)
        y = _sigmoid(ga_ref[rs, :].astype(F32)) * ya
        yb = jnp.dot(ob_ref[rs, :], wb_ref[...], preferred_element_type=F32)
        y = y + _sigmoid(gb_ref[rs, :].astype(F32)) * yb
        ym = jnp.dot(om_ref[rs, :], wm_ref[...], preferred_element_type=F32)
        y = y + _sigmoid(gm_ref[rs, :].astype(F32)) * ym
        y_ref[rs, :] = y.astype(y_ref.dtype)


def _merge(o_a, o_b, o_m, w_a, w_b, w_m, proj, off_ga, off_gb, off_gm, bm=1024, bn=512):
    m = o_a.shape[0]
    n = w_a.shape[1]
    bm = min(bm, m)
    ja, jb, jm = off_ga // bn, off_gb // bn, off_gm // bn

    def row(width):
        return pl.BlockSpec((bm, width), lambda i, j: (i, 0))

    def col(height):
        return pl.BlockSpec((height, bn), lambda i, j: (0, j))

    def gate(j0):
        return pl.BlockSpec((bm, bn), lambda i, j: (i, j0 + j))

    return pl.pallas_call(
        _merge_kernel,
        grid=(m // bm, n // bn),
        in_specs=[row(o_a.shape[1]), row(o_b.shape[1]), row(o_m.shape[1]),
                  col(w_a.shape[0]), col(w_b.shape[0]), col(w_m.shape[0]),
                  gate(ja), gate(jb), gate(jm)],
        out_specs=pl.BlockSpec((bm, bn), lambda i, j: (i, j)),
        out_shape=jax.ShapeDtypeStruct((m, n), BF16),
        compiler_params=_params(("parallel", "arbitrary")),
        name="merge",
    )(o_a, o_b, o_m, w_a, w_b, w_m, proj, proj, proj)


def _t5_bucket_table():
    qi = np.arange(BLOCK)[:, None]
    kj = np.arange(2 * BLOCK)[None, :]
    dist = qi + BLOCK - kj
    band = (dist >= 0) & (dist < WINDOW)
    n = np.maximum(dist, 0)
    max_exact = N_BUCKETS // 2
    nf = np.maximum(n, 1).astype(np.float32)
    large = max_exact + (np.log(nf / np.float32(max_exact)) / np.float32(math.log(MAX_DISTANCE / max_exact))
                         * np.float32(N_BUCKETS - max_exact)).astype(np.int32)
    large = np.minimum(large, N_BUCKETS - 1)
    bucket = np.where(n < max_exact, n, large)
    return np.where(band, bucket, -1).astype(np.int32)


def _bias_table_kernel(rel_ref, idx_ref, o_ref):
    first = pl.program_id(0)
    h = pl.program_id(1)
    idx = idx_ref[...]
    acc = jnp.full(idx.shape, NEG, F32)
    for b in range(N_BUCKETS):
        acc = jnp.where(idx == b, rel_ref[b, h], acc)
    col = lax.broadcasted_iota(jnp.int32, idx.shape, 1)
    o_ref[0, 0] = jnp.where(jnp.logical_and(first == 1, col < BLOCK), NEG, acc)


def _bias_table(rel_bias):
    idx = jnp.asarray(_t5_bucket_table())
    return pl.pallas_call(
        _bias_table_kernel,
        grid=(2, A_HEADS),
        in_specs=[pl.BlockSpec(memory_space=pltpu.SMEM),
                  pl.BlockSpec((BLOCK, 2 * BLOCK), lambda f, h: (0, 0))],
        out_specs=pl.BlockSpec((1, 1, BLOCK, 2 * BLOCK), lambda f, h: (f, h, 0, 0)),
        out_shape=jax.ShapeDtypeStruct((2, A_HEADS, BLOCK, 2 * BLOCK), F32),
        compiler_params=_params(("arbitrary", "arbitrary")),
        name="swa_bias_table",
    )(rel_bias, idx)


def _swa_kernel(sink_ref, q_ref, kc_ref, kp_ref, vc_ref, vp_ref, bias_ref, o_ref):
    assert 2 * A_HEAD_DIM == LANES and A_GROUP % 2 == 0
    n_pair = A_GROUP // 2
    low = lax.broadcasted_iota(jnp.int32, (1, LANES), 1) < A_HEAD_DIM
    scale = A_HEAD_DIM ** -0.5
    ones_low = jnp.where(low, 1.0, 0.0).astype(F32) + jnp.zeros((2 * BLOCK, LANES), F32)
    ones_high = 1.0 - ones_low

    def kv_halves(cur_ref, prev_ref, kh):
        sl = slice((kh // 2) * LANES, (kh // 2 + 1) * LANES)
        w = jnp.concatenate([prev_ref[:, sl], cur_ref[:, sl]], axis=0).astype(F32)
        swapped = pltpu.roll(w, A_HEAD_DIM, axis=1)
        in_low, in_high = (w, swapped) if kh % 2 == 0 else (swapped, w)
        return jnp.where(low, in_low, 0.0), jnp.where(low, 0.0, in_high)

    scores = []
    for kh in range(A_KV_HEADS):
        qg = jnp.concatenate([q_ref[:, (kh * n_pair + m) * LANES:(kh * n_pair + m + 1) * LANES]
                              for m in range(n_pair)], axis=0)
        k_even, k_odd = kv_halves(kc_ref, kp_ref, kh)
        scores.append((_mm_nt(qg, k_even * scale), _mm_nt(qg, k_odd * scale)))

    for kh in range(A_KV_HEADS):
        v_even, v_odd = kv_halves(vc_ref, vp_ref, kh)
        probs = ([], [])
        sink_terms = []
        for m in range(n_pair):
            rows = slice(m * BLOCK, (m + 1) * BLOCK)
            terms = []
            for parity in range(2):
                h = kh * A_GROUP + 2 * m + parity
                s = scores[kh][parity][rows] + bias_ref[0, h]
                mx = jnp.maximum(jnp.max(s, axis=-1, keepdims=True), sink_ref[h])
                probs[parity].append(jnp.exp(s - mx).astype(BF16))
                terms.append(jnp.exp(sink_ref[h] - mx))
            sink_terms.append(jnp.where(low, terms[0], terms[1]))
        res = (_mm(jnp.concatenate(probs[0], axis=0), jnp.concatenate([v_even, ones_low], axis=1))
               + _mm(jnp.concatenate(probs[1], axis=0), jnp.concatenate([v_odd, ones_high], axis=1)))
        for m in range(n_pair):
            rows = slice(m * BLOCK, (m + 1) * BLOCK)
            o = res[rows, :LANES] / (res[rows, LANES:] + sink_terms[m])
            o_ref[:, (kh * n_pair + m) * LANES:(kh * n_pair + m + 1) * LANES] = o.astype(o_ref.dtype)


def _swa_attention(proj, sinks, bias, seq):
    t = proj.shape[0]
    nb = seq // BLOCK
    jk, jv = OFF_AK // A_KV, OFF_AV // A_KV

    def cur(jc):
        return pl.BlockSpec((BLOCK, A_KV), lambda i: (i, jc))

    def prev(jc):
        return pl.BlockSpec((BLOCK, A_KV), lambda i: (jnp.maximum(i - 1, 0), jc))

    return pl.pallas_call(
        _swa_kernel,
        grid=(t // BLOCK,),
        in_specs=[pl.BlockSpec(memory_space=pltpu.SMEM),
                  pl.BlockSpec((BLOCK, A_Q), lambda i: (i, OFF_AQ // A_Q)),
                  cur(jk), prev(jk), cur(jv), prev(jv),
                  pl.BlockSpec((1, A_HEADS, BLOCK, 2 * BLOCK),
                               lambda i: (jnp.where(i % nb == 0, 1, 0), 0, 0, 0))],
        out_specs=pl.BlockSpec((BLOCK, A_Q), lambda i: (i, 0)),
        out_shape=jax.ShapeDtypeStruct((t, A_Q), BF16),
        compiler_params=_params(("parallel",)),
        name="swa_attention",
    )(sinks, proj, proj, proj, proj, proj, bias)


def _xattn_kernel(q_ref, kv_ref, o_ref):
    scale = M_HEAD_DIM ** -0.5
    for h in range(M_HEADS):
        hs = slice(h * M_HEAD_DIM, (h + 1) * M_HEAD_DIM)
        vs = slice(M_Q + h * M_HEAD_DIM, M_Q + (h + 1) * M_HEAD_DIM)
        s = _mm_nt(q_ref[:, hs], kv_ref[:, hs]) * scale
        m = jnp.max(s, axis=-1, keepdims=True)
        p = jnp.exp(s - m)
        denom = jnp.sum(p, axis=-1, keepdims=True)
        o = _mm(p, kv_ref[:, vs]) * (1.0 / denom)
        o_ref[:, hs] = o.astype(o_ref.dtype)


def _cross_attention(gates, mem_kv, seq, mem_len, rows=512):
    t = gates.shape[0]
    rows = min(rows, seq)
    per_batch = seq // rows
    return pl.pallas_call(
        _xattn_kernel,
        grid=(t // rows,),
        in_specs=[pl.BlockSpec((rows, M_Q), lambda i: (i, 0)),
                  pl.BlockSpec((mem_len, 2 * M_Q), lambda i: (i // per_batch, 0))],
        out_specs=pl.BlockSpec((rows, M_Q), lambda i: (i, 0)),
        out_shape=jax.ShapeDtypeStruct((t, M_Q), BF16),
        compiler_params=_params(("parallel",)),
        name="cross_attention",
    )(gates, mem_kv)


def _unit_lower_inverse_minus_eye(a_mats, eye, blk16, blk32):
    assert CHUNK == 64
    nd = [jnp.where(blk16, -a, 0.0) for a in a_mats]
    p = [eye + n for n in nd]
    pw = [_mm(n, n) for n in nd]
    for _ in range(2):
        r = [_mm(jnp.concatenate([w, q], axis=0), w) for w, q in zip(pw, p)]
        pw = [x[:CHUNK] for x in r]
        p = [q + x[CHUNK:] for q, x in zip(p, r)]
    d = [q + _mm(q, w) for q, w in zip(p, pw)]
    for inner, outer in ((blk16, blk32), (blk32, None)):
        if outer is None:
            c_off = [jnp.where(inner, 0.0, a) for a in a_mats]
        else:
            c_off = [jnp.where(jnp.logical_and(outer, jnp.logical_not(inner)), a, 0.0) for a in a_mats]
        x = [_mm(dd, c) for dd, c in zip(d, c_off)]
        d = [dd - _mm(xx, dd) for dd, xx in zip(d, x)]
    return [dd - eye for dd in d]


def _gdn_kernel(q_ref, k_ref, v_ref, z_ref, ba_ref, cwq_ref, cwk_ref, cwv_ref, alog_ref, dtb_ref, gout_ref,
                o_ref, xq_ref, xk_ref, xv_ref, s_ref, gt_ref, *, tt, hb):
    hg = pl.program_id(1)
    j = pl.program_id(2)
    pad = SUBLANES
    nc = tt // CHUNK

    @pl.when(j == 0)
    def _():
        for r in (xq_ref, xk_ref, xv_ref):
            r[0:pad, :] = jnp.zeros((pad, r.shape[1]), F32)
        s_ref[...] = jnp.zeros_like(s_ref)

    @pl.when(j > 0)
    def _():
        for r in (xq_ref, xk_ref, xv_ref):
            r[0:pad, :] = r[tt:tt + pad, :]

    def conv_silu(x_ref, xe_ref, cw_ref):
        xe_ref[pad:pad + tt, :] = x_ref[...].astype(F32)
        w = cw_ref[...]
        xe = xe_ref[0:pad + tt, :]
        taps = [pltpu.roll(xe, CONV_W - 1 - i, axis=0)[pad:] for i in range(CONV_W - 1)] + [xe[pad:]]
        y = taps[0] * w[0:1, :]
        for i in range(1, CONV_W):
            y = y + taps[i] * w[i:i + 1, :]
        return y * _sigmoid(y)

    qs = conv_silu(q_ref, xq_ref, cwq_ref)
    ks = conv_silu(k_ref, xk_ref, cwk_ref)
    vs = conv_silu(v_ref, xv_ref, cwv_ref)

    ba = ba_ref[...]
    lane = lax.broadcasted_iota(jnp.int32, (tt, LANES), 1)
    pos = lax.broadcasted_iota(jnp.int32, (tt, LANES), 0) % CHUNK
    beta_all = _sigmoid(ba)
    xg = ba + dtb_ref[...]
    softplus = jnp.maximum(xg, 0.0) + jnp.log1p(jnp.exp(-jnp.abs(xg)))
    gc_all = -jnp.exp(alog_ref[...]) * softplus
    step = 1
    while step < CHUNK:
        gc_all = gc_all + jnp.where(pos >= step, pltpu.roll(gc_all, step, axis=0), 0.0)
        step *= 2
    gt_ref[...] = gc_all.T

    row = lax.broadcasted_iota(jnp.int32, (CHUNK, CHUNK), 0)
    colm = lax.broadcasted_iota(jnp.int32, (CHUNK, CHUNK), 1)
    tri = row >= colm
    strict = row > colm
    blk16 = (row // 16) == (colm // 16)
    blk32 = (row // 32) == (colm // 32)
    eye = jnp.where(row == colm, 1.0, 0.0).astype(F32)

    def pick(arr, l):
        return jnp.sum(jnp.where(lane == l, arr, 0.0), axis=1, keepdims=True)

    gw = gout_ref[...]
    heads = []
    for i in range(hb):
        h = hg * hb + i
        hs = slice(i * B_DK, (i + 1) * B_DK)
        q = qs[:, hs]
        k = ks[:, hs]
        q = (q * lax.rsqrt(jnp.sum(q * q, axis=-1, keepdims=True) + EPS)) * (B_DK ** -0.5)
        k = k * lax.rsqrt(jnp.sum(k * k, axis=-1, keepdims=True) + EPS)
        beta = pick(beta_all, h)
        gcc = pick(gc_all, B_HEADS + h)
        gcr = gt_ref[pl.ds(B_HEADS + h, 1), :]
        egc = jnp.exp(gcc)
        kb = k * beta
        zf = z_ref[:, hs].astype(F32)
        heads.append(dict(q=q, k=k, kt=k.T, kb=kb, gcc=gcc, gcr=gcr, qe=q * egc,
                          rhs=jnp.concatenate([vs[:, hs] * beta, kb * egc], axis=1),
                          zg=zf * _sigmoid(zf), hs=hs))

    probs = [(i, c) for c in range(nc) for i in range(hb)]
    rsl = [slice(c * CHUNK, (c + 1) * CHUNK) for c in range(nc)]
    decay = [jnp.exp(jnp.where(tri, heads[i]["gcc"][rsl[c]] - heads[i]["gcr"][:, rsl[c]], NEG)) for i, c in probs]
    kq = [_mm_nt(jnp.concatenate([heads[i]["kb"][rsl[c]], heads[i]["q"][rsl[c]]], axis=0), heads[i]["k"][rsl[c]])
          for i, c in probs]
    a_mats = [jnp.where(strict, x[:CHUNK] * dc, 0.0) for x, dc in zip(kq, decay)]
    att = [x[CHUNK:] * dc for x, dc in zip(kq, decay)]
    t_minus_i = _unit_lower_inverse_minus_eye(a_mats, eye, blk16, blk32)
    rhs = [heads[i]["rhs"][rsl[c]] for i, c in probs]
    sol = [r + _mm(tm, r) for tm, r in zip(t_minus_i, rhs)]

    state = [s_ref[i] for i in range(hb)]
    for c in range(nc):
        rs = rsl[c]
        idx = [c * hb + i for i in range(hb)]
        ws = [_mm(jnp.concatenate([sol[p][:, B_DV:], heads[i]["qe"][rs]], axis=0), state[i])
              for i, p in enumerate(idx)]
        v_new = [sol[p][:, :B_DV] - w[:CHUNK] for p, w in zip(idx, ws)]
        g_last = [heads[i]["gcc"][rs][CHUNK - 1:CHUNK, :] for i in range(hb)]
        lhs = [jnp.concatenate([att[p], heads[i]["kt"][:, rs] * jnp.exp(g_last[i] - heads[i]["gcr"][:, rs])], axis=0)
               for i, p in enumerate(idx)]
        r2 = [_mm(l, v) for l, v in zip(lhs, v_new)]
        for i in range(hb):
            state[i] = state[i] * jnp.exp(g_last[i]) + r2[i][CHUNK:]
            o = ws[i][CHUNK:] + r2[i][:CHUNK]
            o = (o * lax.rsqrt(jnp.mean(o * o, axis=-1, keepdims=True) + EPS)) * gw
            o_ref[rs, heads[i]["hs"]] = (o * heads[i]["zg"][rs]).astype(o_ref.dtype)
    for i in range(hb):
        s_ref[i] = state[i]


def _gated_deltanet(proj, ba, conv_w, a_log, dt_bias, g_out, batch, seq, tt=256, hb=4):
    t = proj.shape[0]
    tt = min(tt, seq)
    width = hb * B_DK
    n_t = seq // tt

    def tok(off):
        j0 = off // width
        return pl.BlockSpec((tt, width), lambda b, g, j: (b * n_t + j, j0 + g))

    def cw(off):
        j0 = off // width
        return pl.BlockSpec((CONV_W, width), lambda b, g, j: (0, j0 + g))

    vec = pl.BlockSpec((1, LANES), lambda b, g, j: (0, 0))
    pad16 = jnp.zeros((1, LANES), F32)
    alog = pad16.at[0, B_HEADS:2 * B_HEADS].set(a_log)
    dtb = pad16.at[0, B_HEADS:2 * B_HEADS].set(dt_bias)
    return pl.pallas_call(
        functools.partial(_gdn_kernel, tt=tt, hb=hb),
        grid=(batch, B_HEADS // hb, n_t),
        in_specs=[tok(OFF_BQ), tok(OFF_BK), tok(OFF_BV), tok(OFF_BZ),
                  pl.BlockSpec((tt, LANES), lambda b, g, j: (b * n_t + j, 0)),
                  cw(0), cw(B_QK), cw(2 * B_QK), vec, vec, vec],
        out_specs=pl.BlockSpec((tt, width), lambda b, g, j: (b * n_t + j, g)),
        out_shape=jax.ShapeDtypeStruct((t, B_V), BF16),
        scratch_shapes=[pltpu.VMEM((tt + 2 * SUBLANES, width), F32),
                        pltpu.VMEM((tt + 2 * SUBLANES, width), F32),
                        pltpu.VMEM((tt + 2 * SUBLANES, width), F32),
                        pltpu.VMEM((hb, B_DK, B_DV), F32),
                        pltpu.VMEM((LANES, tt), F32)],
        compiler_params=_params(("parallel", "parallel", "arbitrary")),
        name="gated_deltanet",
    )(proj, proj, proj, proj, ba, conv_w, conv_w, conv_w, alog, dtb, g_out.reshape(1, B_DV))


def _split_w_in(w):
    wt = jnp.swapaxes(w, 0, 1).astype(BF16)
    return wt, wt[OFF_MQ + 2 * B_HEADS:]


def kernel(x, mem, rel_bias, g_mix, w_in, conv_w, a_log, dt_bias, g_dn_out, sinks, g_mem, w_mem_kv,
           w_br_a, w_br_b, w_br_m, w_o, g_ffn, w_ffn_in, w_ffn_out, g_final):
    batch, seq, d = x.shape
    mem_len = mem.shape[1]
    depth = w_in.shape[0]
    d_ff = w_ffn_out.shape[1]

    xt = x.reshape(batch * seq, d)
    memt = mem.reshape(batch * mem_len, d)
    bias = _bias_table(rel_bias)
    for l in range(depth):
        wt_tok, wt_gate = _split_w_in(w_in[l])
        h, ba = _rmsnorm_ba(xt, g_mix[l], wt_tok, OFF_MQ)
        proj, (wo, wa, wb, wm, w2) = _matmul_nt(
            h, wt_tok, BF16, PROJ_TOK_BN, n=OFF_MQ,
            sides=((w_o[l], 32), (w_br_a[l], 16), (w_br_b[l], 16), (w_br_m[l], 16), (w_ffn_out[l], 128)))
        gates, (w1,) = _matmul_nt(
            h, wt_gate, BF16, PROJ_GATE_BN, n=M_Q + 3 * d, sides=((w_ffn_in[l], 32),))

        o_a = _swa_attention(proj, sinks[l], bias, seq)
        o_b = _gated_deltanet(proj, ba, conv_w[l], a_log[l], dt_bias[l], g_dn_out[l], batch, seq)
        mem_kv = _matmul(_rmsnorm(memt, g_mem[l], BF16), w_mem_kv[l].astype(BF16), BF16, bn=512)
        o_m = _cross_attention(gates, mem_kv, seq, mem_len)

        y = _merge(o_a, o_b, o_m, wa, wb, wm, gates, M_Q, M_Q + d, M_Q + 2 * d)
        xt, xg, r = _matmul_residual_prenorm(y, wo, xt, g_ffn[l])

        act = _swiglu_in(xg, r, w1, d_ff)
        xt = _matmul_residual(act, w2, xt, bm=512, bn=512)
    out = _rmsnorm(xt, g_final, F32)
    return out.reshape(batch, seq, d)
```
